```python
import jax, jax.numpy as jnp
from jax import lax
import numpy as np

D_MODEL = 1024
BATCH = 8
SEQ = 4096
DEPTH = 4

GRID_W = 64
CTX_LEN = 256
N_Q_HEADS = 8
N_KV_HEADS = 2
HEAD_DIM = 64
D_ATTN = N_Q_HEADS * HEAD_DIM
D_KV = N_KV_HEADS * HEAD_DIM
Q_BLOCK = 128
ROPE_THETA = 10000.0
D_CONF = 512
CONF_WIDTH = 31
D_SHORT = 512
SHORT_WIDTH = 3
D_POOL = 512
POOL_WINDOWS = (2, 4, 8, 16)
POOL_GROUP = D_POOL // len(POOL_WINDOWS)
N_EXPERTS = 16
CAPACITY_FACTOR = 2
D_EXPERT = 2048
EPS = 1e-6
EVEN_IN = 2 * D_CONF + D_ATTN + 2 * D_KV
EVEN_MIX = D_CONF + D_ATTN
ODD_IN = 3 * D_SHORT + D_POOL
ODD_MIX = D_SHORT + D_POOL

kernel_name = 'hybrid_flow_backbone'


def rmsnorm(x, g):
    x32 = x.astype(jnp.float32)
    y = x32 * lax.rsqrt(jnp.mean(x32 * x32, axis=-1, keepdims=True) + EPS)
    return y * g.astype(jnp.float32)


def modulate(x, g, shift, scale):
    y = rmsnorm(x, g) * (1.0 + scale.astype(jnp.float32)) + shift.astype(jnp.float32)
    return y.astype(x.dtype)


def rms_heads(x, g):
    return rmsnorm(x, g).astype(x.dtype)


def rope_tables(n_tok):
    rows = n_tok // GRID_W
    row = jnp.repeat(jnp.arange(rows), GRID_W)
    col = jnp.tile(jnp.arange(GRID_W), rows)
    n_freq = HEAD_DIM // 4
    inv = ROPE_THETA ** (-jnp.arange(n_freq, dtype=jnp.float32) / n_freq)
    ang_r = row.astype(jnp.float32)[:, None] * inv
    ang_c = col.astype(jnp.float32)[:, None] * inv
    return (jnp.cos(ang_r), jnp.sin(ang_r), jnp.cos(ang_c), jnp.sin(ang_c))


def rope_2d(x, cos_r, sin_r, cos_c, sin_c):
    x32 = x.astype(jnp.float32)
    half = HEAD_DIM // 2
    quarter = HEAD_DIM // 4

    def rot(xp, cos, sin):
        x1, x2 = xp[..., :quarter], xp[..., quarter:]
        cs, sn = cos[None, :, None, :], sin[None, :, None, :]
        return jnp.concatenate([x1 * cs - x2 * sn, x2 * cs + x1 * sn], axis=-1)

    out = jnp.concatenate([rot(x32[..., :half], cos_r, sin_r), rot(x32[..., half:], cos_c, sin_c)], axis=-1)
    return out.astype(x.dtype)


def block_attention(q, k, v):
    b, lq, hq, hd = q.shape
    hkv = k.shape[2]
    grp = hq // hkv
    nb = lq // Q_BLOCK
    qb = q.reshape(b, nb, Q_BLOCK, hkv, grp, hd).transpose(1, 0, 2, 3, 4, 5)
    scale = hd ** -0.5

    def one_block(qi):
        s = jnp.einsum('bqhgd,bnhd->bhgqn', qi, k).astype(jnp.float32) * scale
        p = jax.nn.softmax(s, axis=-1).astype(v.dtype)
        return jnp.einsum('bhgqn,bnhd->bqhgd', p, v)

    o = lax.map(one_block, qb)
    return o.transpose(1, 0, 2, 3, 4, 5).reshape(b, lq, hq * hd)


def depthwise_conv(u, w):
    pad = w.shape[0] // 2
    return lax.conv_general_dilated(u, w.astype(u.dtype)[:, None, :], window_strides=(1,),
                                    padding=[(pad, pad)], dimension_numbers=('NWC', 'WIO', 'NWC'),
                                    feature_group_count=u.shape[-1])


def conformer_conv(a, g, conv_w, conv_b, ln_g, ln_b):
    u = a * jax.nn.sigmoid(g)
    u = depthwise_conv(u, conv_w) + conv_b
    u32 = u.astype(jnp.float32)
    mu = jnp.mean(u32, axis=-1, keepdims=True)
    var = jnp.mean(jnp.square(u32 - mu), axis=-1, keepdims=True)
    un = (u32 - mu) * lax.rsqrt(var + EPS) * ln_g.astype(jnp.float32) + ln_b.astype(jnp.float32)
    return jax.nn.silu(un).astype(a.dtype)


def split_heads(t, n_heads):
    return t.reshape(t.shape[0], t.shape[1], n_heads, HEAD_DIM)


def even_mixer(h_lat, h_ctx, w_in, conv_w, conv_b, ln_g, ln_b, q_g, k_g, w_out, rope, with_ctx_out):
    kv0 = 2 * D_CONF + D_ATTN
    p = h_lat @ w_in
    a, g, q, k, v = jnp.split(p, [D_CONF, 2 * D_CONF, kv0, kv0 + D_KV], axis=-1)
    pc = h_ctx @ (w_in if with_ctx_out else w_in[:, kv0:])
    kc = rms_heads(split_heads(pc[..., -2 * D_KV:-D_KV], N_KV_HEADS), k_g)
    vc = split_heads(pc[..., -D_KV:], N_KV_HEADS)
    q = rope_2d(rms_heads(split_heads(q, N_Q_HEADS), q_g), *rope)
    k = rope_2d(rms_heads(split_heads(k, N_KV_HEADS), k_g), *rope)
    v = split_heads(v, N_KV_HEADS)
    attn = block_attention(q, jnp.concatenate([kc, k], axis=1), jnp.concatenate([vc, v], axis=1))
    conv = conformer_conv(a, g, conv_w, conv_b, ln_g, ln_b)
    y_lat = jnp.concatenate([conv, attn], axis=-1) @ w_out
    y_ctx = None
    if with_ctx_out:
        ac, gc, qc = jnp.split(pc[..., :kv0], [D_CONF, 2 * D_CONF], axis=-1)
        qc = rms_heads(split_heads(qc, N_Q_HEADS), q_g)
        attn_c = block_attention(qc, kc, vc)
        conv_c = conformer_conv(ac, gc, conv_w, conv_b, ln_g, ln_b)
        y_ctx = jnp.concatenate([conv_c, attn_c], axis=-1) @ w_out
    return y_lat, y_ctx


def multiscale_pool(u, pool_w, pool_scale):
    b, n, ch = u.shape
    u32 = u.astype(jnp.float32)
    csum = jnp.concatenate([jnp.zeros((b, 1, ch), jnp.float32), jnp.cumsum(u32, axis=1)], axis=1)
    t = np.arange(n)
    outs = []
    for gi, w in enumerate(POOL_WINDOWS):
        lo = np.clip(t - w // 2, 0, n)
        hi = np.clip(t + w // 2, 0, n)
        cnt = jnp.asarray((hi - lo).astype(np.float32))
        sl = slice(gi * POOL_GROUP, (gi + 1) * POOL_GROUP)
        cg = csum[..., sl]
        mean = (cg[:, hi] - cg[:, lo]) / cnt[None, :, None]
        diff = (mean - u32[..., sl]).astype(u.dtype)
        outs.append(diff @ pool_w[gi])
    return jnp.concatenate(outs, axis=-1) * pool_scale


def odd_mixer(h, w_in, conv_w, pool_w, pool_scale, w_out):
    u, gate_b, gate_c, pin = jnp.split(h @ w_in, [D_SHORT, 2 * D_SHORT, 3 * D_SHORT], axis=-1)
    short = gate_b * depthwise_conv(gate_c * u, conv_w)
    pool = multiscale_pool(pin, pool_w, pool_scale)
    return jnp.concatenate([short, pool], axis=-1) @ w_out


def expert_choice_ffn(h, w_router, w_gate, w_up, w_down):
    n_tok, d = h.shape[1], h.shape[2]
    cap = CAPACITY_FACTOR * n_tok // N_EXPERTS
    aff = jax.nn.softmax(jnp.einsum('bnd,de->bne', h, w_router).astype(jnp.float32), axis=-1)
    top_aff, top_idx = lax.top_k(jnp.swapaxes(aff, 1, 2), cap)
    xe = jax.vmap(lambda hb, ib: hb[ib])(h, top_idx)
    g = jnp.einsum('becd,edf->becf', xe, w_gate)
    u = jnp.einsum('becd,edf->becf', xe, w_up)
    ye = jnp.einsum('becf,efd->becd', jax.nn.silu(g) * u, w_down) * top_aff[..., None].astype(h.dtype)

    def combine(ib, yb):
        return jnp.zeros((n_tok, d), yb.dtype).at[ib.reshape(-1)].add(yb.reshape(-1, d))

    return jax.vmap(combine)(top_idx, ye)


def setup_inputs(seed: int = 0) -> dict:
    key = jax.random.key(seed)
    ks = iter(jax.random.split(key, 40))
    n_even = (DEPTH + 1) // 2
    n_odd = DEPTH // 2
    f32 = jnp.float32

    def nrm(shape, s):
        return jax.random.normal(next(ks), shape, f32) * s

    d = D_MODEL
    return {
        'x': nrm((BATCH, SEQ, d), 1.0),
        'c': nrm((BATCH, d), 1.0),
        'ctx': nrm((BATCH, CTX_LEN, d), 1.0),
        'c_ctx': nrm((d,), 1.0),
        'norm_mix_g': 1.0 + nrm((DEPTH, d), 0.05),
        'norm_ffn_g': 1.0 + nrm((DEPTH, d), 0.05),
        'w_mod': nrm((DEPTH, d, 6 * d), 0.5 * d ** -0.5),
        'b_mod': nrm((DEPTH, 6 * d), 0.01),
        'ev_w_in': nrm((n_even, d, EVEN_IN), d ** -0.5),
        'ev_conv_w': nrm((n_even, CONF_WIDTH, D_CONF), CONF_WIDTH ** -0.5),
        'ev_conv_b': nrm((n_even, D_CONF), 0.01),
        'ev_ln_g': 1.0 + nrm((n_even, D_CONF), 0.05),
        'ev_ln_b': nrm((n_even, D_CONF), 0.01),
        'ev_q_norm_g': 1.0 + nrm((n_even, HEAD_DIM), 0.05),
        'ev_k_norm_g': 1.0 + nrm((n_even, HEAD_DIM), 0.05),
        'ev_w_out': nrm((n_even, EVEN_MIX, d), EVEN_MIX ** -0.5),
        'od_w_in': nrm((n_odd, d, ODD_IN), d ** -0.5),
        'od_conv_w': nrm((n_odd, SHORT_WIDTH, D_SHORT), SHORT_WIDTH ** -0.5),
        'od_pool_w': nrm((n_odd, len(POOL_WINDOWS), POOL_GROUP, POOL_GROUP), POOL_GROUP ** -0.5),
        'od_pool_scale': 1.0 + nrm((n_odd, D_POOL), 0.1),
        'od_w_out': nrm((n_odd, ODD_MIX, d), ODD_MIX ** -0.5),
        'w_router': nrm((DEPTH, d, N_EXPERTS), d ** -0.5),
        'w_gate': nrm((DEPTH, N_EXPERTS, d, D_EXPERT), d ** -0.5),
        'w_up': nrm((DEPTH, N_EXPERTS, d, D_EXPERT), d ** -0.5),
        'w_down': nrm((DEPTH, N_EXPERTS, D_EXPERT, d), D_EXPERT ** -0.5),
    }


def reference(x, c, ctx, c_ctx, norm_mix_g, norm_ffn_g, w_mod, b_mod,
              ev_w_in, ev_conv_w, ev_conv_b, ev_ln_g, ev_ln_b, ev_q_norm_g, ev_k_norm_g, ev_w_out,
              od_w_in, od_conv_w, od_pool_w, od_pool_scale, od_w_out,
              w_router, w_gate, w_up, w_down):
    rope = rope_tables(x.shape[1])
    last_even = ((DEPTH - 1) // 2) * 2
    for i in range(DEPTH):
        j = i // 2
        is_even = i % 2 == 0
        ctx_live = i < last_even
        mods = jnp.split(jax.nn.silu(c) @ w_mod[i] + b_mod[i], 6, axis=-1)
        sh1, sc1, g1, sh2, sc2, g2 = [t[:, None, :] for t in mods]
        if is_even or ctx_live:
            sh1c, sc1c, g1c, sh2c, sc2c, g2c = jnp.split(jax.nn.silu(c_ctx) @ w_mod[i] + b_mod[i], 6, axis=-1)
        h = modulate(x, norm_mix_g[i], sh1, sc1)
        if is_even:
            hc = modulate(ctx, norm_mix_g[i], sh1c, sc1c)
            y, yc = even_mixer(h, hc, ev_w_in[j], ev_conv_w[j], ev_conv_b[j], ev_ln_g[j], ev_ln_b[j],
                               ev_q_norm_g[j], ev_k_norm_g[j], ev_w_out[j], rope, ctx_live)
        else:
            y = odd_mixer(h, od_w_in[j], od_conv_w[j], od_pool_w[j], od_pool_scale[j], od_w_out[j])
            yc = None
            if ctx_live:
                hc = modulate(ctx, norm_mix_g[i], sh1c, sc1c)
                yc = odd_mixer(hc, od_w_in[j], od_conv_w[j], od_pool_w[j], od_pool_scale[j], od_w_out[j])
        x = x + g1 * y
        x = x + g2 * expert_choice_ffn(modulate(x, norm_ffn_g[i], sh2, sc2),
                                       w_router[i], w_gate[i], w_up[i], w_down[i])
        if ctx_live:
            ctx = ctx + g1c * yc
            ctx = ctx + g2c * expert_choice_ffn(modulate(ctx, norm_ffn_g[i], sh2c, sc2c),
                                                w_router[i], w_gate[i], w_up[i], w_down[i])
    return x
```

```python
import functools
import math

import jax
import jax.numpy as jnp
import numpy as np
from jax import lax
from jax.experimental import pallas as pl
from jax.experimental.pallas import tpu as pltpu

F32 = jnp.float32
BF16 = jnp.bfloat16

GRID_W = 64
N_Q_HEADS = 8
N_KV_HEADS = 2
HEAD_DIM = 64
D_ATTN = N_Q_HEADS * HEAD_DIM
D_KV = N_KV_HEADS * HEAD_DIM
ROPE_THETA = 10000.0
D_CONF = 512
D_SHORT = 512
D_POOL = 512
POOL_WINDOWS = (2, 4, 8, 16)
POOL_GROUP = D_POOL // len(POOL_WINDOWS)
N_EXPERTS = 16
CAPACITY_FACTOR = 2
EPS = 1e-6

LANES = 128
HALO = 16
VMEM_LIMIT = 56 * 1024 * 1024


def _sigmoid(x):
    return 1.0 / (1.0 + jnp.exp(-x))


def _dot(a, b):
    return jnp.dot(a, b, preferred_element_type=F32)


def _split_bf16(x):
    hi = x.astype(BF16)
    lo = (x - hi.astype(F32)).astype(BF16)
    return hi, lo


def _cparams(*sem):
    return pltpu.CompilerParams(dimension_semantics=sem, vmem_limit_bytes=VMEM_LIMIT)


def _mod_kernel(c_ref, w_ref, b_ref, o_ref):
    c = c_ref[...]
    s_hi, s_lo = _split_bf16(c * _sigmoid(c))
    w_hi, w_lo = _split_bf16(w_ref[0])
    o_ref[0] = _dot(s_hi, w_hi) + _dot(s_hi, w_lo) + _dot(s_lo, w_hi) + b_ref[0]


def _modulations(c_rows, w_mod, b_mod):
    depth, d, n = w_mod.shape
    r = c_rows.shape[0]
    tn = 1536
    return pl.pallas_call(
        _mod_kernel,
        grid=(depth, n // tn),
        in_specs=[pl.BlockSpec((r, d), lambda i, j: (0, 0)),
                  pl.BlockSpec((1, d, tn), lambda i, j: (i, 0, j)),
                  pl.BlockSpec((1, 1, tn), lambda i, j: (i, 0, j))],
        out_specs=pl.BlockSpec((1, r, tn), lambda i, j: (i, 0, j)),
        out_shape=jax.ShapeDtypeStruct((depth, r, n), F32),
        compiler_params=_cparams("arbitrary", "arbitrary"),
        name="modulations",
    )(c_rows, w_mod, b_mod.reshape(depth, 1, n))


def _modulated_norm(x, g, sh, sc):
    ms = jnp.mean(x * x, axis=-1, keepdims=True)
    return x * lax.rsqrt(ms + EPS) * g * (1.0 + sc) + sh


def _post(x, y, g1, nf_g, sh2, sc2, wr_hi, wr_lo, xo_ref, h2_ref, aff_ref):
    xn = x + g1 * y
    xo_ref[0] = xn
    h = _modulated_norm(xn, nf_g, sh2, sc2)
    h_hi, h_lo = _split_bf16(h)
    h2_ref[0] = h_hi
    lg = _dot(h_hi, wr_hi) + _dot(h_hi, wr_lo) + _dot(h_lo, wr_hi)
    lgt = lg.T[:N_EXPERTS]
    ex = jnp.exp(lgt - jnp.max(lgt, axis=0, keepdims=True))
    aff_ref[0] = ex / jnp.sum(ex, axis=0, keepdims=True)


def _head_rms(t, gain, bd):
    hi, lo = _split_bf16(t * t)
    ms = _dot(hi, bd) + _dot(lo, bd)
    return t * lax.rsqrt(ms + EPS) * gain


def _rope(t, cos, sin_a, sin_b):
    w = t.shape[-1]
    q = HEAD_DIM // 4
    return t * cos + pltpu.roll(t, w - q, 1) * sin_a + pltpu.roll(t, q, 1) * sin_b


def _even_in_kernel(*refs, rope):
    if rope:
        (x_ref, sh_ref, sc_ref, g_ref, w_ref, qg_ref, kg_ref, bd_ref, cos_ref, sa_ref, sb_ref,
         u_ref, q_ref, k_ref, v_ref) = refs
    else:
        x_ref, sh_ref, sc_ref, g_ref, w_ref, qg_ref, kg_ref, bd_ref, u_ref, q_ref, k_ref, v_ref = refs
    h = _modulated_norm(x_ref[0], g_ref[...], sh_ref[0], sc_ref[0])
    p = _dot(h.astype(BF16), w_ref[...])
    kv0 = 2 * D_CONF + D_ATTN
    u_ref[0] = p[:, :D_CONF] * _sigmoid(p[:, D_CONF:2 * D_CONF])
    q = _head_rms(p[:, 2 * D_CONF:kv0], qg_ref[...], bd_ref[...])
    k = _head_rms(p[:, kv0:kv0 + D_KV], kg_ref[...], bd_ref[:D_KV, :D_KV])
    if rope:
        cos, sa, sb = cos_ref[...], sa_ref[...], sb_ref[...]
        q = _rope(q, cos, sa, sb)
        k = _rope(k, cos[:, :D_KV], sa[:, :D_KV], sb[:, :D_KV])
    q_ref[0] = (q * (HEAD_DIM ** -0.5)).astype(BF16)
    k_ref[0] = k.astype(BF16)
    v_ref[0] = p[:, kv0 + D_KV:].astype(BF16)


def _even_in(x, sh, sc, g, w_bf, qg, kg, bd, rope_tabs):
    b, l, d = x.shape
    tm = min(512, l)
    n_in = w_bf.shape[1]
    row = lambda bi, t: (bi, 0, 0)
    tok = lambda bi, t: (bi, t, 0)
    const = lambda bi, t: (0, 0)
    in_specs = [pl.BlockSpec((1, tm, d), tok), pl.BlockSpec((1, 1, d), row), pl.BlockSpec((1, 1, d), row),
                pl.BlockSpec((1, d), const), pl.BlockSpec((d, n_in), const),
                pl.BlockSpec((1, D_ATTN), const), pl.BlockSpec((1, D_KV), const),
                pl.BlockSpec((D_ATTN, D_ATTN), const)]
    args = [x, sh, sc, g, w_bf, qg, kg, bd]
    if rope_tabs is not None:
        in_specs += [pl.BlockSpec((tm, D_ATTN), lambda bi, t: (t, 0))] * 3
        args += list(rope_tabs)
    return pl.pallas_call(
        functools.partial(_even_in_kernel, rope=rope_tabs is not None),
        grid=(b, l // tm),
        in_specs=in_specs,
        out_specs=[pl.BlockSpec((1, tm, D_CONF), tok), pl.BlockSpec((1, tm, D_ATTN), tok),
                   pl.BlockSpec((1, tm, D_KV), tok), pl.BlockSpec((1, tm, D_KV), tok)],
        out_shape=[jax.ShapeDtypeStruct((b, l, D_CONF), F32), jax.ShapeDtypeStruct((b, l, D_ATTN), BF16),
                   jax.ShapeDtypeStruct((b, l, D_KV), BF16), jax.ShapeDtypeStruct((b, l, D_KV), BF16)],
        compiler_params=_cparams("arbitrary", "arbitrary"),
        name="even_in_proj",
    )(*args)


def _fill_halo(xs_ref, prev_ref, cur_ref, next_ref, tm, nt):
    t = pl.program_id(1)
    xs_ref[0:HALO] = jnp.where(t > 0, prev_ref[0], 0.0)
    xs_ref[HALO:HALO + tm] = cur_ref[0]
    xs_ref[HALO + tm:HALO + tm + HALO] = jnp.where(t < nt - 1, next_ref[0], 0.0)


def _conf_conv_kernel(up_ref, uc_ref, un_ref, w_ref, b_ref, lg_ref, lb_ref, o_ref, xs_ref, *, tm, nt, rc):
    _fill_halo(xs_ref, up_ref, uc_ref, un_ref, tm, nt)
    width = w_ref.shape[0]
    pad = width // 2
    for r in range(tm // rc):
        acc = jnp.zeros((rc, D_CONF), F32)
        for k in range(width):
            acc = acc + xs_ref[pl.ds(r * rc + HALO - pad + k, rc), :] * w_ref[k:k + 1, :]
        u = acc + b_ref[...]
        mu = jnp.mean(u, axis=-1, keepdims=True)
        ctr = u - mu
        var = jnp.mean(ctr * ctr, axis=-1, keepdims=True)
        un = ctr * lax.rsqrt(var + EPS) * lg_ref[...] + lb_ref[...]
        o_ref[0, r * rc:(r + 1) * rc, :] = (un * _sigmoid(un)).astype(BF16)


def _halo_specs(tm, l, c):
    hb = tm // HALO
    last = l // HALO - 1
    return [pl.BlockSpec((1, HALO, c), lambda bi, t: (bi, jnp.maximum(t * hb - 1, 0), 0)),
            pl.BlockSpec((1, tm, c), lambda bi, t: (bi, t, 0)),
            pl.BlockSpec((1, HALO, c), lambda bi, t: (bi, jnp.minimum((t + 1) * hb, last), 0))]


def _conf_conv(u, conv_w, conv_b, ln_g, ln_b):
    b, l, c = u.shape
    tm = min(256, l)
    nt = l // tm
    const = lambda bi, t: (0, 0)
    return pl.pallas_call(
        functools.partial(_conf_conv_kernel, tm=tm, nt=nt, rc=32),
        grid=(b, nt),
        in_specs=_halo_specs(tm, l, c) + [pl.BlockSpec(conv_w.shape, const)] + [pl.BlockSpec((1, c), const)] * 3,
        out_specs=pl.BlockSpec((1, tm, c), lambda bi, t: (bi, t, 0)),
        out_shape=jax.ShapeDtypeStruct((b, l, c), BF16),
        scratch_shapes=[pltpu.VMEM((tm + 2 * HALO, c), F32)],
        compiler_params=_cparams("arbitrary", "arbitrary"),
        name="conformer_conv",
    )(u, u, u, conv_w, conv_b, ln_g, ln_b)


def _attn_kernel(q_ref, kt_ref, v_ref, o_ref, *, chunks):
    q = q_ref[0]
    tq = q.shape[0]
    out = None
    for half in range(2):
        m = jnp.full((tq, 1), -jnp.inf, F32)
        l = jnp.zeros((tq, 1), F32)
        acc = jnp.zeros((tq, LANES), F32)
        for c0, cs in chunks:
            s = _dot(q, kt_ref[0, 0, half, :, c0:c0 + cs])
            m_new = jnp.maximum(m, jnp.max(s, axis=1, keepdims=True))
            alpha = jnp.exp(m - m_new)
            p = jnp.exp(s - m_new)
            l = alpha * l + jnp.sum(p, axis=1, keepdims=True)
            acc = alpha * acc + _dot(p.astype(BF16), v_ref[0, 0, half, c0:c0 + cs, :])
            m = m_new
        o = acc / l
        out = o if out is None else out + o
    o_ref[0] = out.astype(BF16)


def _kv_layouts(k, v):
    b, lk, _ = k.shape
    kt = k.reshape(b, lk, N_KV_HEADS, HEAD_DIM).transpose(0, 2, 3, 1)
    z = jnp.zeros_like(kt)
    kt = jnp.stack([jnp.concatenate([kt, z], axis=2), jnp.concatenate([z, kt], axis=2)], axis=2)
    vh = v.reshape(b, lk, N_KV_HEADS, HEAD_DIM).transpose(0, 2, 1, 3)
    zv = jnp.zeros_like(vh)
    vv = jnp.stack([jnp.concatenate([vh, zv], axis=3), jnp.concatenate([zv, vh], axis=3)], axis=2)
    return kt, vv


def _key_chunks(lk, size=1024):
    head = lk % size
    chunks = [(0, head)] if head else []
    return tuple(chunks + [(c, size) for c in range(head, lk, size)])


def _attention(q, k, v):
    b, l, _ = q.shape
    lk = k.shape[1]
    kt, vv = _kv_layouts(k, v)
    tq = min(512, l)
    n_pairs = D_ATTN // LANES
    pairs_per_kv = n_pairs // N_KV_HEADS
    return pl.pallas_call(
        functools.partial(_attn_kernel, chunks=_key_chunks(lk)),
        grid=(b, n_pairs, l // tq),
        in_specs=[pl.BlockSpec((1, tq, LANES), lambda bi, j, t: (bi, t, j)),
                  pl.BlockSpec((1, 1, 2, LANES, lk), lambda bi, j, t: (bi, j // pairs_per_kv, 0, 0, 0)),
                  pl.BlockSpec((1, 1, 2, lk, LANES), lambda bi, j, t: (bi, j // pairs_per_kv, 0, 0, 0))],
        out_specs=pl.BlockSpec((1, tq, LANES), lambda bi, j, t: (bi, t, j)),
        out_shape=jax.ShapeDtypeStruct((b, l, D_ATTN), BF16),
        compiler_params=_cparams("arbitrary", "arbitrary", "arbitrary"),
        name="attention",
    )(q, kt, vv)


def _even_out_kernel(cv_ref, at_ref, wo_ref, x_ref, g1_ref, nfg_ref, sh2_ref, sc2_ref, wrh_ref, wrl_ref,
                     xo_ref, h2_ref, aff_ref):
    y = _dot(cv_ref[0], wo_ref[:D_CONF, :]) + _dot(at_ref[0], wo_ref[D_CONF:, :])
    _post(x_ref[0], y, g1_ref[0], nfg_ref[...], sh2_ref[0], sc2_ref[0], wrh_ref[...], wrl_ref[...],
          xo_ref, h2_ref, aff_ref)


def _post_specs(tm, d):
    row = lambda bi, t: (bi, 0, 0)
    const = lambda bi, t: (0, 0)
    in_specs = [pl.BlockSpec((1, tm, d), lambda bi, t: (bi, t, 0)), pl.BlockSpec((1, 1, d), row),
                pl.BlockSpec((1, d), const), pl.BlockSpec((1, 1, d), row), pl.BlockSpec((1, 1, d), row),
                pl.BlockSpec((d, LANES), const), pl.BlockSpec((d, LANES), const)]
    out_specs = [pl.BlockSpec((1, tm, d), lambda bi, t: (bi, t, 0)), pl.BlockSpec((1, tm, d), lambda bi, t: (bi, t, 0)),
                 pl.BlockSpec((1, N_EXPERTS, tm), lambda bi, t: (bi, 0, t))]
    return in_specs, out_specs


def _post_out_shapes(b, l, d):
    return [jax.ShapeDtypeStruct((b, l, d), F32), jax.ShapeDtypeStruct((b, l, d), BF16),
            jax.ShapeDtypeStruct((b, N_EXPERTS, l), F32)]


def _even_out(conv, attn, wo_bf, x, g1, nf_g, sh2, sc2, wr_hi, wr_lo):
    b, l, d = x.shape
    tm = min(512, l)
    tok = lambda bi, t: (bi, t, 0)
    post_in, post_out = _post_specs(tm, d)
    return pl.pallas_call(
        _even_out_kernel,
        grid=(b, l // tm),
        in_specs=[pl.BlockSpec((1, tm, D_CONF), tok), pl.BlockSpec((1, tm, D_ATTN), tok),
                  pl.BlockSpec(wo_bf.shape, lambda bi, t: (0, 0))] + post_in,
        out_specs=post_out,
        out_shape=_post_out_shapes(b, l, d),
        compiler_params=_cparams("arbitrary", "arbitrary"),
        name="even_out_proj",
    )(conv, attn, wo_bf, x, g1, nf_g, sh2, sc2, wr_hi, wr_lo)


def _odd_in_kernel(x_ref, sh_ref, sc_ref, g_ref, w_ref, z_ref):
    h = _modulated_norm(x_ref[0], g_ref[...], sh_ref[0], sc_ref[0])
    p = _dot(h.astype(BF16), w_ref[...])
    ds_ = D_SHORT
    z_ref[0, :, :ds_] = p[:, 2 * ds_:3 * ds_] * p[:, :ds_]
    z_ref[0, :, ds_:2 * ds_] = p[:, 3 * ds_:]
    z_ref[0, :, 2 * ds_:] = p[:, ds_:2 * ds_]


def _odd_in(x, sh, sc, g, w_bf):
    b, l, d = x.shape
    tm = min(512, l)
    row = lambda bi, t: (bi, 0, 0)
    tok = lambda bi, t: (bi, t, 0)
    const = lambda bi, t: (0, 0)
    nz = 2 * D_SHORT + D_POOL
    return pl.pallas_call(
        _odd_in_kernel,
        grid=(b, l // tm),
        in_specs=[pl.BlockSpec((1, tm, d), tok), pl.BlockSpec((1, 1, d), row), pl.BlockSpec((1, 1, d), row),
                  pl.BlockSpec((1, d), const), pl.BlockSpec(w_bf.shape, const)],
        out_specs=pl.BlockSpec((1, tm, nz), tok),
        out_shape=jax.ShapeDtypeStruct((b, l, nz), F32),
        compiler_params=_cparams("arbitrary", "arbitrary"),
        name="odd_in_proj",
    )(x, sh, sc, g, w_bf)


def _odd_mix_kernel(zp_ref, zc_ref, zn_ref, cw_ref, pw_ref, ps_ref, wo_ref,
                    x_ref, g1_ref, nfg_ref, sh2_ref, sc2_ref, wrh_ref, wrl_ref,
                    xo_ref, h2_ref, aff_ref, xs_ref, *, tm, nt, n_tok):
    _fill_halo(xs_ref, zp_ref, zc_ref, zn_ref, tm, nt)
    ds_ = D_SHORT
    width = cw_ref.shape[0]
    pad = width // 2
    conv = jnp.zeros((tm, ds_), F32)
    for k in range(width):
        conv = conv + xs_ref[pl.ds(HALO - pad + k, tm), 0:ds_] * cw_ref[k:k + 1, :]
    short = xs_ref[pl.ds(HALO, tm), 2 * ds_:3 * ds_] * conv
    pos = pl.program_id(1) * tm + lax.broadcasted_iota(jnp.int32, (tm, POOL_GROUP), 0)
    pooled = []
    for gi, w in enumerate(POOL_WINDOWS):
        c0 = ds_ + gi * POOL_GROUP
        tot = jnp.zeros((tm, POOL_GROUP), F32)
        for dlt in range(-(w // 2), w // 2):
            tot = tot + xs_ref[pl.ds(HALO + dlt, tm), c0:c0 + POOL_GROUP]
        cnt = (jnp.minimum(pos + w // 2, n_tok) - jnp.maximum(pos - w // 2, 0)).astype(F32)
        diff = tot / cnt - xs_ref[pl.ds(HALO, tm), c0:c0 + POOL_GROUP]
        pooled.append(_dot(diff.astype(BF16), pw_ref[gi]))
    pool = jnp.concatenate(pooled, axis=-1) * ps_ref[...]
    y = _dot(short.astype(BF16), wo_ref[:ds_, :]) + _dot(pool.astype(BF16), wo_ref[ds_:, :])
    _post(x_ref[0], y, g1_ref[0], nfg_ref[...], sh2_ref[0], sc2_ref[0], wrh_ref[...], wrl_ref[...],
          xo_ref, h2_ref, aff_ref)


def _odd_mix(z, conv_w, pool_w_bf, pool_scale, wo_bf, x, g1, nf_g, sh2, sc2, wr_hi, wr_lo):
    b, l, d = x.shape
    tm = min(256, l)
    nt = l // tm
    nz = z.shape[-1]
    const = lambda bi, t: (0, 0)
    post_in, post_out = _post_specs(tm, d)
    return pl.pallas_call(
        functools.partial(_odd_mix_kernel, tm=tm, nt=nt, n_tok=l),
        grid=(b, nt),
        in_specs=_halo_specs(tm, l, nz) + [pl.BlockSpec(conv_w.shape, const),
                                            pl.BlockSpec(pool_w_bf.shape, lambda bi, t: (0, 0, 0)),
                                            pl.BlockSpec((1, D_POOL), const),
                                            pl.BlockSpec(wo_bf.shape, const)] + post_in,
        out_specs=post_out,
        out_shape=_post_out_shapes(b, l, d),
        scratch_shapes=[pltpu.VMEM((tm + 2 * HALO, nz), F32)],
        compiler_params=_cparams("arbitrary", "arbitrary"),
        name="odd_mixer",
    )(z, z, z, conv_w, pool_w_bf, pool_scale, wo_bf, x, g1, nf_g, sh2, sc2, wr_hi, wr_lo)


def _route_kernel(aff_ref, tri_ref, pos_ref, *, cap):
    key = lax.bitcast_convert_type(aff_ref[0], jnp.int32)
    n = key.shape[1]
    capf = float(cap)
    thr = jnp.zeros((N_EXPERTS, 1), jnp.int32)
    for bit in range(30, -1, -1):
        cand = thr | (1 << bit)
        cnt = jnp.sum(jnp.where(key >= cand, 1.0, 0.0), axis=1, keepdims=True)
        thr = jnp.where(cnt >= capf, cand, thr)
    n_gt = jnp.sum(jnp.where(key > thr, 1.0, 0.0), axis=1, keepdims=True)
    need = capf - n_gt
    tri = tri_ref[...]
    off_eq = jnp.zeros((N_EXPERTS, 1), F32)
    off_sel = jnp.zeros((N_EXPERTS, 1), F32)
    for j in range(n // LANES):
        kj = key[:, j * LANES:(j + 1) * LANES]
        gt = kj > thr
        eqf = jnp.where(kj == thr, 1.0, 0.0)
        rank = _dot(eqf.astype(BF16), tri) + off_eq - eqf
        self_ = jnp.where(gt, 1.0, jnp.where(rank < need, eqf, 0.0))
        slot = _dot(self_.astype(BF16), tri) + off_sel - 1.0
        pos_ref[0, :, j * LANES:(j + 1) * LANES] = jnp.where(self_ > 0.0, slot, -1.0)
        off_eq = off_eq + jnp.sum(eqf, axis=1, keepdims=True)
        off_sel = off_sel + jnp.sum(self_, axis=1, keepdims=True)


def _route(aff_t, cap):
    b, e, n = aff_t.shape
    tri = jnp.asarray(np.triu(np.ones((LANES, LANES), np.float32)), BF16)
    return pl.pallas_call(
        functools.partial(_route_kernel, cap=cap),
        grid=(b,),
        in_specs=[pl.BlockSpec((1, e, n), lambda bi: (bi, 0, 0)), pl.BlockSpec((LANES, LANES), lambda bi: (0, 0))],
        out_specs=pl.BlockSpec((1, e, n), lambda bi: (bi, 0, 0)),
        out_shape=jax.ShapeDtypeStruct((b, e, n), F32),
        compiler_params=_cparams("arbitrary"),
        name="route",
    )(aff_t, tri)


def _gather_kernel(pos_ref, aff_ref, h_ref, xe_ref, gate_ref, *, cap, tc):
    pos = pos_ref[0, 0]
    aff = aff_ref[0, 0]
    n = pos.shape[1]
    slot = lax.broadcasted_iota(jnp.int32, (cap, tc), 0).astype(F32)
    acc = jnp.zeros((cap, h_ref.shape[2]), F32)
    gate = jnp.zeros((cap, 1), F32)
    for c in range(n // tc):
        hit = slot == pos[:, c * tc:(c + 1) * tc]
        acc = acc + _dot(jnp.where(hit, 1.0, 0.0).astype(BF16), h_ref[0, c * tc:(c + 1) * tc, :])
        gate = gate + jnp.sum(jnp.where(hit, aff[:, c * tc:(c + 1) * tc], 0.0), axis=1, keepdims=True)
    xe_ref[0, 0] = acc.astype(BF16)
    gate_ref[0, 0] = jnp.broadcast_to(gate, (cap, LANES))


def _gather(pos, aff_t, h2, cap):
    b, e, n = pos.shape
    d = h2.shape[-1]
    tc = min(512, n)
    sel = lambda bi, ei: (bi, ei, 0, 0)
    return pl.pallas_call(
        functools.partial(_gather_kernel, cap=cap, tc=tc),
        grid=(b, e),
        in_specs=[pl.BlockSpec((1, 1, 1, n), sel), pl.BlockSpec((1, 1, 1, n), sel),
                  pl.BlockSpec((1, n, d), lambda bi, ei: (bi, 0, 0))],
        out_specs=[pl.BlockSpec((1, 1, cap, d), lambda bi, ei: (ei, bi, 0, 0)),
                   pl.BlockSpec((1, 1, cap, LANES), lambda bi, ei: (ei, bi, 0, 0))],
        out_shape=[jax.ShapeDtypeStruct((e, b, cap, d), BF16), jax.ShapeDtypeStruct((e, b, cap, LANES), F32)],
        compiler_params=_cparams("arbitrary", "arbitrary"),
        name="moe_gather",
    )(pos.reshape(b, e, 1, n), aff_t.reshape(b, e, 1, n), h2)


def _ffn_kernel(xe_ref, gate_ref, wg_ref, wu_ref, wd_ref, ye_ref, acc_ref, *, nf):
    f = pl.program_id(2)
    x = xe_ref[0]
    g = _dot(x, wg_ref[0, 0].astype(BF16))
    u = _dot(x, wu_ref[0, 0].astype(BF16))
    contrib = _dot((g * _sigmoid(g) * u).astype(BF16), wd_ref[0, 0].astype(BF16))

    @pl.when(f == 0)
    def _():
        acc_ref[...] = contrib

    @pl.when(f > 0)
    def _():
        acc_ref[...] += contrib

    @pl.when(f == nf - 1)
    def _():
        gate = jnp.tile(gate_ref[0], (1, acc_ref.shape[1] // LANES))
        ye_ref[0] = (acc_ref[...] * gate).astype(BF16)


def _expert_ffn(xe, gate, w_gate, w_up, w_down, layer):
    e, m, d = xe.shape
    dff = w_gate.shape[-1]
    tm = min(1024, m)
    tf = 512
    nf = dff // tf
    return pl.pallas_call(
        functools.partial(_ffn_kernel, nf=nf),
        grid=(e, m // tm, nf),
        in_specs=[pl.BlockSpec((1, tm, d), lambda ei, mi, f: (ei, mi, 0)),
                  pl.BlockSpec((1, tm, LANES), lambda ei, mi, f: (ei, mi, 0)),
                  pl.BlockSpec((1, 1, d, tf), lambda ei, mi, f: (layer, ei, 0, f)),
                  pl.BlockSpec((1, 1, d, tf), lambda ei, mi, f: (layer, ei, 0, f)),
                  pl.BlockSpec((1, 1, tf, d), lambda ei, mi, f: (layer, ei, f, 0))],
        out_specs=pl.BlockSpec((1, tm, d), lambda ei, mi, f: (ei, mi, 0)),
        out_shape=jax.ShapeDtypeStruct((e, m, d), BF16),
        scratch_shapes=[pltpu.VMEM((tm, d), F32)],
        compiler_params=_cparams("arbitrary", "arbitrary", "arbitrary"),
        name="expert_ffn",
    )(xe, gate, w_gate, w_up, w_down)


def _combine_kernel(posc_ref, ye_ref, x_ref, g2_ref, o_ref, acc_ref, *, cap):
    e = pl.program_id(2)
    posc = posc_ref[0]
    tn = posc.shape[0]
    lane = lax.broadcasted_iota(jnp.int32, posc.shape, 1)
    col = jnp.sum(jnp.where(lane == e, posc, 0.0), axis=1, keepdims=True)
    slot = lax.broadcasted_iota(jnp.int32, (tn, cap), 1).astype(F32)
    contrib = _dot(jnp.where(col == slot, 1.0, 0.0).astype(BF16), ye_ref[0, 0])

    @pl.when(e == 0)
    def _():
        acc_ref[...] = contrib

    @pl.when(e > 0)
    def _():
        acc_ref[...] += contrib

    @pl.when(e == N_EXPERTS - 1)
    def _():
        o_ref[0] = x_ref[0] + g2_ref[0] * acc_ref[...]


def _combine(pos, ye, x, g2, cap):
    b, n, d = x.shape
    e = pos.shape[1]
    tn = min(1024, n)
    posc = jnp.swapaxes(pos, 1, 2)
    tok = lambda bi, t, ei: (bi, t, 0)
    return pl.pallas_call(
        functools.partial(_combine_kernel, cap=cap),
        grid=(b, n // tn, e),
        in_specs=[pl.BlockSpec((1, tn, e), tok),
                  pl.BlockSpec((1, 1, cap, d), lambda bi, t, ei: (ei, bi, 0, 0)),
                  pl.BlockSpec((1, tn, d), tok),
                  pl.BlockSpec((1, 1, d), lambda bi, t, ei: (bi, 0, 0))],
        out_specs=pl.BlockSpec((1, tn, d), tok),
        out_shape=jax.ShapeDtypeStruct((b, n, d), F32),
        scratch_shapes=[pltpu.VMEM((tn, d), F32)],
        compiler_params=_cparams("arbitrary", "arbitrary", "arbitrary"),
        name="moe_combine",
    )(posc, ye, x, g2)


def _moe(x, h2, aff_t, g2, w_gate, w_up, w_down, layer):
    b, n, d = x.shape
    cap = CAPACITY_FACTOR * n // N_EXPERTS
    pos = _route(aff_t, cap)
    xe, gate = _gather(pos, aff_t, h2, cap)
    ye = _expert_ffn(xe.reshape(N_EXPERTS, b * cap, d), gate.reshape(N_EXPERTS, b * cap, LANES),
                     w_gate, w_up, w_down, layer)
    return _combine(pos, ye.reshape(N_EXPERTS, b, cap, d), x, g2, cap)


def _rope_tables(n_tok):
    rows = n_tok // GRID_W
    row = np.repeat(np.arange(rows), GRID_W).astype(np.float32)
    col = np.tile(np.arange(GRID_W), rows).astype(np.float32)
    n_freq = HEAD_DIM // 4
    inv = jnp.asarray(ROPE_THETA, F32) ** (-jnp.arange(n_freq, dtype=F32) / n_freq)
    ang_r = jnp.asarray(row)[:, None] * inv
    ang_c = jnp.asarray(col)[:, None] * inv
    cr, sr, cc, sc = jnp.cos(ang_r), jnp.sin(ang_r), jnp.cos(ang_c), jnp.sin(ang_c)
    zero = jnp.zeros_like(sr)
    cos = jnp.concatenate([cr, cr, cc, cc], axis=-1)
    sin_a = jnp.concatenate([-sr, zero, -sc, zero], axis=-1)
    sin_b = jnp.concatenate([zero, sr, zero, sc], axis=-1)
    return tuple(jnp.tile(t, (1, N_Q_HEADS)) for t in (cos, sin_a, sin_b))


def _head_mean_matrix():
    blk = np.kron(np.eye(N_Q_HEADS, dtype=np.float32), np.full((HEAD_DIM, HEAD_DIM), 1.0 / HEAD_DIM, np.float32))
    return jnp.asarray(blk, BF16)


def _rows(m, b):
    return m[:b, None, :]


def kernel(x, c, ctx, c_ctx, norm_mix_g, norm_ffn_g, w_mod, b_mod, ev_w_in, ev_conv_w, ev_conv_b, ev_ln_g, ev_ln_b, ev_q_norm_g, ev_k_norm_g, ev_w_out, od_w_in, od_conv_w, od_pool_w, od_pool_scale, od_w_out, w_router, w_gate, w_up, w_down):
    b, l, d = x.shape
    depth = w_mod.shape[0]
    last_even = ((depth - 1) // 2) * 2

    n_rows = -(-(b + 1) // 8) * 8
    c_rows = jnp.zeros((n_rows, d), F32).at[:b].set(c).at[b].set(c_ctx)
    mods = _modulations(c_rows, w_mod, b_mod)
    rope = _rope_tables(l)
    bd = _head_mean_matrix()

    for i in range(depth):
        j = i // 2
        is_even = i % 2 == 0
        ctx_live = i < last_even
        m6 = mods[i].reshape(n_rows, 6, d)
        sh1, sc1, g1, sh2, sc2, g2 = [_rows(m6[:, t], b) for t in range(6)]
        mc = [jnp.broadcast_to(m6[b, t][None, None, :], (b, 1, d)) for t in range(6)]
        sh1c, sc1c, g1c, sh2c, sc2c, g2c = mc
        nm_g = norm_mix_g[i][None, :]
        nf_g = norm_ffn_g[i][None, :]
        wr_hi, wr_lo = _split_bf16(jnp.pad(w_router[i], ((0, 0), (0, LANES - N_EXPERTS))))

        if is_even:
            w_in = ev_w_in[j].astype(BF16)
            wo = ev_w_out[j].astype(BF16)
            qg = jnp.tile(ev_q_norm_g[j], N_Q_HEADS)[None, :]
            kg = jnp.tile(ev_k_norm_g[j], N_KV_HEADS)[None, :]
            conv_args = (ev_conv_w[j], ev_conv_b[j][None, :], ev_ln_g[j][None, :], ev_ln_b[j][None, :])
            u, q, k, v = _even_in(x, sh1, sc1, nm_g, w_in, qg, kg, bd, rope)
            uc, qc, kc, vc = _even_in(ctx, sh1c, sc1c, nm_g, w_in, qg, kg, bd, None)
            attn = _attention(q, jnp.concatenate([kc, k], axis=1), jnp.concatenate([vc, v], axis=1))
            conv = _conf_conv(u, *conv_args)
            x, h2, aff = _even_out(conv, attn, wo, x, g1, nf_g, sh2, sc2, wr_hi, wr_lo)
            if ctx_live:
                attn_c = _attention(qc, kc, vc)
                conv_c = _conf_conv(uc, *conv_args)
                ctx, h2c, affc = _even_out(conv_c, attn_c, wo, ctx, g1c, nf_g, sh2c, sc2c, wr_hi, wr_lo)
        else:
            w_in = od_w_in[j].astype(BF16)
            wo = od_w_out[j].astype(BF16)
            mix_args = (od_conv_w[j], od_pool_w[j].astype(BF16), od_pool_scale[j][None, :], wo)
            z = _odd_in(x, sh1, sc1, nm_g, w_in)
            x, h2, aff = _odd_mix(z, *mix_args, x, g1, nf_g, sh2, sc2, wr_hi, wr_lo)
            if ctx_live:
                zc = _odd_in(ctx, sh1c, sc1c, nm_g, w_in)
                ctx, h2c, affc = _odd_mix(zc, *mix_args, ctx, g1c, nf_g, sh2c, sc2c, wr_hi, wr_lo)
        x = _moe(x, h2, aff, g2, w_gate, w_up, w_down, i)
        if ctx_live:
            ctx = _moe(ctx, h2c, affc, g2c, w_gate, w_up, w_down, i)
    return x
```

```python
import functools
import math

import jax
import jax.numpy as jnp
import numpy as np
from jax import lax
from jax.experimental import pallas as pl
from jax.experimental.pallas import tpu as pltpu

F32 = jnp.float32
BF16 = jnp.bfloat16

GRID_W = 64
N_Q_HEADS = 8
N_KV_HEADS = 2
HEAD_DIM = 64
D_ATTN = N_Q_HEADS * HEAD_DIM
D_KV = N_KV_HEADS * HEAD_DIM
ROPE_THETA = 10000.0
D_CONF = 512
D_SHORT = 512
D_POOL = 512
POOL_WINDOWS = (2, 4, 8, 16)
POOL_GROUP = D_POOL // len(POOL_WINDOWS)
N_EXPERTS = 16
CAPACITY_FACTOR = 2
EPS = 1e-6

LANES = 128
HALO = 16
VMEM_LIMIT = 56 * 1024 * 1024


def _sigmoid(x):
    return 1.0 / (1.0 + jnp.exp(-x))


def _dot(a, b):
    return jnp.dot(a, b, preferred_element_type=F32)


def _split_bf16(x):
    hi = x.astype(BF16)
    lo = (x - hi.astype(F32)).astype(BF16)
    return hi, lo


def _cparams(*sem):
    return pltpu.CompilerParams(dimension_semantics=sem, vmem_limit_bytes=VMEM_LIMIT)


def _mod_kernel(c_ref, w_ref, b_ref, o_ref):
    c = c_ref[...]
    s_hi, s_lo = _split_bf16(c * _sigmoid(c))
    w_hi, w_lo = _split_bf16(w_ref[0])
    o_ref[0] = _dot(s_hi, w_hi) + _dot(s_hi, w_lo) + _dot(s_lo, w_hi) + b_ref[0]


def _modulations(c_rows, w_mod, b_mod):
    depth, d, n = w_mod.shape
    r = c_rows.shape[0]
    tn = 1536
    return pl.pallas_call(
        _mod_kernel,
        grid=(depth, n // tn),
        in_specs=[pl.BlockSpec((r, d), lambda i, j: (0, 0)),
                  pl.BlockSpec((1, d, tn), lambda i, j: (i, 0, j)),
                  pl.BlockSpec((1, 1, tn), lambda i, j: (i, 0, j))],
        out_specs=pl.BlockSpec((1, r, tn), lambda i, j: (i, 0, j)),
        out_shape=jax.ShapeDtypeStruct((depth, r, n), F32),
        compiler_params=_cparams("arbitrary", "arbitrary"),
        name="modulations",
    )(c_rows, w_mod, b_mod.reshape(depth, 1, n))


def _modulated_norm(x, g, sh, sc):
    ms = jnp.mean(x * x, axis=-1, keepdims=True)
    return x * lax.rsqrt(ms + EPS) * g * (1.0 + sc) + sh


def _store_token_major(ref3, val):
    rows, d = val.shape
    nk = d // LANES
    for k in range(nk):
        ref3[0, pl.ds(k, rows, stride=nk), :] = val[:, k * LANES:(k + 1) * LANES]


def _post(x, y, g1, nf_g, sh2, sc2, wr_hi, wr_lo, xo_ref, h3_ref, aff_ref):
    xn = x + g1 * y
    xo_ref[0] = xn
    h = _modulated_norm(xn, nf_g, sh2, sc2)
    h_hi, h_lo = _split_bf16(h)
    _store_token_major(h3_ref, h)
    lg = _dot(h_hi, wr_hi) + _dot(h_hi, wr_lo) + _dot(h_lo, wr_hi)
    lgt = lg.T[:N_EXPERTS]
    ex = jnp.exp(lgt - jnp.max(lgt, axis=0, keepdims=True))
    aff_ref[0] = ex / jnp.sum(ex, axis=0, keepdims=True)


def _head_rms(t, gain, bd):
    hi, lo = _split_bf16(t * t)
    ms = _dot(hi, bd) + _dot(lo, bd)
    return t * lax.rsqrt(ms + EPS) * gain


def _rope(t, cos, sin_a, sin_b):
    w = t.shape[-1]
    q = HEAD_DIM // 4
    return t * cos + pltpu.roll(t, w - q, 1) * sin_a + pltpu.roll(t, q, 1) * sin_b


def _even_in_kernel(*refs, rope):
    if rope:
        (x_ref, sh_ref, sc_ref, g_ref, w_ref, qg_ref, kg_ref, bd_ref, cos_ref, sa_ref, sb_ref,
         u_ref, q_ref, k_ref, v_ref) = refs
    else:
        x_ref, sh_ref, sc_ref, g_ref, w_ref, qg_ref, kg_ref, bd_ref, u_ref, q_ref, k_ref, v_ref = refs
    h = _modulated_norm(x_ref[0], g_ref[...], sh_ref[0], sc_ref[0])
    p = _dot(h.astype(BF16), w_ref[...])
    kv0 = 2 * D_CONF + D_ATTN
    u_ref[0] = p[:, :D_CONF] * _sigmoid(p[:, D_CONF:2 * D_CONF])
    q = _head_rms(p[:, 2 * D_CONF:kv0], qg_ref[...], bd_ref[...])
    k = _head_rms(p[:, kv0:kv0 + D_KV], kg_ref[...], bd_ref[:D_KV, :D_KV])
    if rope:
        cos, sa, sb = cos_ref[...], sa_ref[...], sb_ref[...]
        q = _rope(q, cos, sa, sb)
        k = _rope(k, cos[:, :D_KV], sa[:, :D_KV], sb[:, :D_KV])
    q_ref[0] = (q * (HEAD_DIM ** -0.5)).astype(BF16)
    k_ref[0] = k.astype(BF16)
    v_ref[0] = p[:, kv0 + D_KV:].astype(BF16)


def _even_in(x, sh, sc, g, w_bf, qg, kg, bd, rope_tabs):
    b, l, d = x.shape
    tm = min(512, l)
    n_in = w_bf.shape[1]
    row = lambda bi, t: (bi, 0, 0)
    tok = lambda bi, t: (bi, t, 0)
    const = lambda bi, t: (0, 0)
    in_specs = [pl.BlockSpec((1, tm, d), tok), pl.BlockSpec((1, 1, d), row), pl.BlockSpec((1, 1, d), row),
                pl.BlockSpec((1, d), const), pl.BlockSpec((d, n_in), const),
                pl.BlockSpec((1, D_ATTN), const), pl.BlockSpec((1, D_KV), const),
                pl.BlockSpec((D_ATTN, D_ATTN), const)]
    args = [x, sh, sc, g, w_bf, qg, kg, bd]
    if rope_tabs is not None:
        in_specs += [pl.BlockSpec((tm, D_ATTN), lambda bi, t: (t, 0))] * 3
        args += list(rope_tabs)
    return pl.pallas_call(
        functools.partial(_even_in_kernel, rope=rope_tabs is not None),
        grid=(b, l // tm),
        in_specs=in_specs,
        out_specs=[pl.BlockSpec((1, tm, D_CONF), tok), pl.BlockSpec((1, tm, D_ATTN), tok),
                   pl.BlockSpec((1, tm, D_KV), tok), pl.BlockSpec((1, tm, D_KV), tok)],
        out_shape=[jax.ShapeDtypeStruct((b, l, D_CONF), F32), jax.ShapeDtypeStruct((b, l, D_ATTN), BF16),
                   jax.ShapeDtypeStruct((b, l, D_KV), BF16), jax.ShapeDtypeStruct((b, l, D_KV), BF16)],
        compiler_params=_cparams("arbitrary", "arbitrary"),
        name="even_in_proj",
    )(*args)


def _fill_halo(xs_ref, prev_ref, cur_ref, next_ref, tm, nt):
    t = pl.program_id(1)
    xs_ref[0:HALO] = jnp.where(t > 0, prev_ref[0], 0.0)
    xs_ref[HALO:HALO + tm] = cur_ref[0]
    xs_ref[HALO + tm:HALO + tm + HALO] = jnp.where(t < nt - 1, next_ref[0], 0.0)


def _conf_conv_kernel(up_ref, uc_ref, un_ref, w_ref, b_ref, lg_ref, lb_ref, o_ref, xs_ref, *, tm, nt, rc):
    _fill_halo(xs_ref, up_ref, uc_ref, un_ref, tm, nt)
    width = w_ref.shape[0]
    pad = width // 2
    for r in range(tm // rc):
        acc = jnp.zeros((rc, D_CONF), F32)
        for k in range(width):
            acc = acc + xs_ref[pl.ds(r * rc + HALO - pad + k, rc), :] * w_ref[k:k + 1, :]
        u = acc + b_ref[...]
        mu = jnp.mean(u, axis=-1, keepdims=True)
        ctr = u - mu
        var = jnp.mean(ctr * ctr, axis=-1, keepdims=True)
        un = ctr * lax.rsqrt(var + EPS) * lg_ref[...] + lb_ref[...]
        o_ref[0, r * rc:(r + 1) * rc, :] = (un * _sigmoid(un)).astype(BF16)


def _halo_specs(tm, l, c):
    hb = tm // HALO
    last = l // HALO - 1
    return [pl.BlockSpec((1, HALO, c), lambda bi, t: (bi, jnp.maximum(t * hb - 1, 0), 0)),
            pl.BlockSpec((1, tm, c), lambda bi, t: (bi, t, 0)),
            pl.BlockSpec((1, HALO, c), lambda bi, t: (bi, jnp.minimum((t + 1) * hb, last), 0))]


def _conf_conv(u, conv_w, conv_b, ln_g, ln_b):
    b, l, c = u.shape
    tm = min(256, l)
    nt = l // tm
    const = lambda bi, t: (0, 0)
    return pl.pallas_call(
        functools.partial(_conf_conv_kernel, tm=tm, nt=nt, rc=32),
        grid=(b, nt),
        in_specs=_halo_specs(tm, l, c) + [pl.BlockSpec(conv_w.shape, const)] + [pl.BlockSpec((1, c), const)] * 3,
        out_specs=pl.BlockSpec((1, tm, c), lambda bi, t: (bi, t, 0)),
        out_shape=jax.ShapeDtypeStruct((b, l, c), BF16),
        scratch_shapes=[pltpu.VMEM((tm + 2 * HALO, c), F32)],
        compiler_params=_cparams("arbitrary", "arbitrary"),
        name="conformer_conv",
    )(u, u, u, conv_w, conv_b, ln_g, ln_b)


def _attn_kernel(q_ref, kt_ref, v_ref, o_ref, *, chunks):
    q = q_ref[0]
    tq = q.shape[0]
    out = None
    for half in range(2):
        m = jnp.full((tq, 1), -jnp.inf, F32)
        l = jnp.zeros((tq, 1), F32)
        acc = jnp.zeros((tq, LANES), F32)
        for c0, cs in chunks:
            s = _dot(q, kt_ref[0, 0, half, :, c0:c0 + cs])
            m_new = jnp.maximum(m, jnp.max(s, axis=1, keepdims=True))
            alpha = jnp.exp(m - m_new)
            p = jnp.exp(s - m_new)
            l = alpha * l + jnp.sum(p, axis=1, keepdims=True)
            acc = alpha * acc + _dot(p.astype(BF16), v_ref[0, 0, half, c0:c0 + cs, :])
            m = m_new
        o = acc / l
        out = o if out is None else out + o
    o_ref[0] = out.astype(BF16)


def _kv_layouts(k, v):
    b, lk, _ = k.shape
    kt = k.reshape(b, lk, N_KV_HEADS, HEAD_DIM).transpose(0, 2, 3, 1)
    z = jnp.zeros_like(kt)
    kt = jnp.stack([jnp.concatenate([kt, z], axis=2), jnp.concatenate([z, kt], axis=2)], axis=2)
    vh = v.reshape(b, lk, N_KV_HEADS, HEAD_DIM).transpose(0, 2, 1, 3)
    zv = jnp.zeros_like(vh)
    vv = jnp.stack([jnp.concatenate([vh, zv], axis=3), jnp.concatenate([zv, vh], axis=3)], axis=2)
    return kt, vv


def _key_chunks(lk, size=1024):
    head = lk % size
    chunks = [(0, head)] if head else []
    return tuple(chunks + [(c, size) for c in range(head, lk, size)])


def _attention(q, k, v):
    b, l, _ = q.shape
    lk = k.shape[1]
    kt, vv = _kv_layouts(k, v)
    tq = min(512, l)
    n_pairs = D_ATTN // LANES
    pairs_per_kv = n_pairs // N_KV_HEADS
    return pl.pallas_call(
        functools.partial(_attn_kernel, chunks=_key_chunks(lk)),
        grid=(b, n_pairs, l // tq),
        in_specs=[pl.BlockSpec((1, tq, LANES), lambda bi, j, t: (bi, t, j)),
                  pl.BlockSpec((1, 1, 2, LANES, lk), lambda bi, j, t: (bi, j // pairs_per_kv, 0, 0, 0)),
                  pl.BlockSpec((1, 1, 2, lk, LANES), lambda bi, j, t: (bi, j // pairs_per_kv, 0, 0, 0))],
        out_specs=pl.BlockSpec((1, tq, LANES), lambda bi, j, t: (bi, t, j)),
        out_shape=jax.ShapeDtypeStruct((b, l, D_ATTN), BF16),
        compiler_params=_cparams("arbitrary", "arbitrary", "arbitrary"),
        name="attention",
    )(q, kt, vv)


def _even_out_kernel(cv_ref, at_ref, wo_ref, x_ref, g1_ref, nfg_ref, sh2_ref, sc2_ref, wrh_ref, wrl_ref,
                     xo_ref, h2_ref, aff_ref):
    y = _dot(cv_ref[0], wo_ref[:D_CONF, :]) + _dot(at_ref[0], wo_ref[D_CONF:, :])
    _post(x_ref[0], y, g1_ref[0], nfg_ref[...], sh2_ref[0], sc2_ref[0], wrh_ref[...], wrl_ref[...],
          xo_ref, h2_ref, aff_ref)


def _post_specs(tm, d):
    row = lambda bi, t: (bi, 0, 0)
    const = lambda bi, t: (0, 0)
    in_specs = [pl.BlockSpec((1, tm, d), lambda bi, t: (bi, t, 0)), pl.BlockSpec((1, 1, d), row),
                pl.BlockSpec((1, d), const), pl.BlockSpec((1, 1, d), row), pl.BlockSpec((1, 1, d), row),
                pl.BlockSpec((d, LANES), const), pl.BlockSpec((d, LANES), const)]
    nk = d // LANES
    out_specs = [pl.BlockSpec((1, tm, d), lambda bi, t: (bi, t, 0)),
                 pl.BlockSpec((1, tm * nk, LANES), lambda bi, t: (bi, t, 0)),
                 pl.BlockSpec((1, N_EXPERTS, tm), lambda bi, t: (bi, 0, t))]
    return in_specs, out_specs


def _post_out_shapes(b, l, d):
    return [jax.ShapeDtypeStruct((b, l, d), F32), jax.ShapeDtypeStruct((b, l * (d // LANES), LANES), F32),
            jax.ShapeDtypeStruct((b, N_EXPERTS, l), F32)]


def _even_out(conv, attn, wo_bf, x, g1, nf_g, sh2, sc2, wr_hi, wr_lo):
    b, l, d = x.shape
    tm = min(512, l)
    tok = lambda bi, t: (bi, t, 0)
    post_in, post_out = _post_specs(tm, d)
    return pl.pallas_call(
        _even_out_kernel,
        grid=(b, l // tm),
        in_specs=[pl.BlockSpec((1, tm, D_CONF), tok), pl.BlockSpec((1, tm, D_ATTN), tok),
                  pl.BlockSpec(wo_bf.shape, lambda bi, t: (0, 0))] + post_in,
        out_specs=post_out,
        out_shape=_post_out_shapes(b, l, d),
        compiler_params=_cparams("arbitrary", "arbitrary"),
        name="even_out_proj",
    )(conv, attn, wo_bf, x, g1, nf_g, sh2, sc2, wr_hi, wr_lo)


def _odd_in_kernel(x_ref, sh_ref, sc_ref, g_ref, w_ref, z_ref):
    h = _modulated_norm(x_ref[0], g_ref[...], sh_ref[0], sc_ref[0])
    p = _dot(h.astype(BF16), w_ref[...])
    ds_ = D_SHORT
    z_ref[0, :, :ds_] = p[:, 2 * ds_:3 * ds_] * p[:, :ds_]
    z_ref[0, :, ds_:2 * ds_] = p[:, 3 * ds_:]
    z_ref[0, :, 2 * ds_:] = p[:, ds_:2 * ds_]


def _odd_in(x, sh, sc, g, w_bf):
    b, l, d = x.shape
    tm = min(512, l)
    row = lambda bi, t: (bi, 0, 0)
    tok = lambda bi, t: (bi, t, 0)
    const = lambda bi, t: (0, 0)
    nz = 2 * D_SHORT + D_POOL
    return pl.pallas_call(
        _odd_in_kernel,
        grid=(b, l // tm),
        in_specs=[pl.BlockSpec((1, tm, d), tok), pl.BlockSpec((1, 1, d), row), pl.BlockSpec((1, 1, d), row),
                  pl.BlockSpec((1, d), const), pl.BlockSpec(w_bf.shape, const)],
        out_specs=pl.BlockSpec((1, tm, nz), tok),
        out_shape=jax.ShapeDtypeStruct((b, l, nz), F32),
        compiler_params=_cparams("arbitrary", "arbitrary"),
        name="odd_in_proj",
    )(x, sh, sc, g, w_bf)


def _odd_mix_kernel(zp_ref, zc_ref, zn_ref, cw_ref, pw_ref, ps_ref, wo_ref,
                    x_ref, g1_ref, nfg_ref, sh2_ref, sc2_ref, wrh_ref, wrl_ref,
                    xo_ref, h2_ref, aff_ref, xs_ref, *, tm, nt, n_tok):
    _fill_halo(xs_ref, zp_ref, zc_ref, zn_ref, tm, nt)
    ds_ = D_SHORT
    width = cw_ref.shape[0]
    pad = width // 2
    conv = jnp.zeros((tm, ds_), F32)
    for k in range(width):
        conv = conv + xs_ref[pl.ds(HALO - pad + k, tm), 0:ds_] * cw_ref[k:k + 1, :]
    short = xs_ref[pl.ds(HALO, tm), 2 * ds_:3 * ds_] * conv
    pos = pl.program_id(1) * tm + lax.broadcasted_iota(jnp.int32, (tm, POOL_GROUP), 0)
    pooled = []
    for gi, w in enumerate(POOL_WINDOWS):
        c0 = ds_ + gi * POOL_GROUP
        tot = jnp.zeros((tm, POOL_GROUP), F32)
        for dlt in range(-(w // 2), w // 2):
            tot = tot + xs_ref[pl.ds(HALO + dlt, tm), c0:c0 + POOL_GROUP]
        cnt = (jnp.minimum(pos + w // 2, n_tok) - jnp.maximum(pos - w // 2, 0)).astype(F32)
        diff = tot / cnt - xs_ref[pl.ds(HALO, tm), c0:c0 + POOL_GROUP]
        pooled.append(_dot(diff.astype(BF16), pw_ref[gi]))
    pool = jnp.concatenate(pooled, axis=-1) * ps_ref[...]
    y = _dot(short.astype(BF16), wo_ref[:ds_, :]) + _dot(pool.astype(BF16), wo_ref[ds_:, :])
    _post(x_ref[0], y, g1_ref[0], nfg_ref[...], sh2_ref[0], sc2_ref[0], wrh_ref[...], wrl_ref[...],
          xo_ref, h2_ref, aff_ref)


def _odd_mix(z, conv_w, pool_w_bf, pool_scale, wo_bf, x, g1, nf_g, sh2, sc2, wr_hi, wr_lo):
    b, l, d = x.shape
    tm = min(256, l)
    nt = l // tm
    nz = z.shape[-1]
    const = lambda bi, t: (0, 0)
    post_in, post_out = _post_specs(tm, d)
    return pl.pallas_call(
        functools.partial(_odd_mix_kernel, tm=tm, nt=nt, n_tok=l),
        grid=(b, nt),
        in_specs=_halo_specs(tm, l, nz) + [pl.BlockSpec(conv_w.shape, const),
                                            pl.BlockSpec(pool_w_bf.shape, lambda bi, t: (0, 0, 0)),
                                            pl.BlockSpec((1, D_POOL), const),
                                            pl.BlockSpec(wo_bf.shape, const)] + post_in,
        out_specs=post_out,
        out_shape=_post_out_shapes(b, l, d),
        scratch_shapes=[pltpu.VMEM((tm + 2 * HALO, nz), F32)],
        compiler_params=_cparams("arbitrary", "arbitrary"),
        name="odd_mixer",
    )(z, z, z, conv_w, pool_w_bf, pool_scale, wo_bf, x, g1, nf_g, sh2, sc2, wr_hi, wr_lo)


def _route_kernel(aff_ref, tri_ref, pos_ref, *, cap):
    key = lax.bitcast_convert_type(aff_ref[0], jnp.int32)
    n = key.shape[1]
    capf = float(cap)
    thr = jnp.zeros((N_EXPERTS, 1), jnp.int32)
    for bit in range(30, -1, -1):
        cand = thr | (1 << bit)
        cnt = jnp.sum(jnp.where(key >= cand, 1.0, 0.0), axis=1, keepdims=True)
        thr = jnp.where(cnt >= capf, cand, thr)
    n_gt = jnp.sum(jnp.where(key > thr, 1.0, 0.0), axis=1, keepdims=True)
    need = capf - n_gt
    tri = tri_ref[...]
    off_eq = jnp.zeros((N_EXPERTS, 1), F32)
    off_sel = jnp.zeros((N_EXPERTS, 1), F32)
    for j in range(n // LANES):
        kj = key[:, j * LANES:(j + 1) * LANES]
        gt = kj > thr
        eqf = jnp.where(kj == thr, 1.0, 0.0)
        rank = _dot(eqf.astype(BF16), tri) + off_eq - eqf
        self_ = jnp.where(gt, 1.0, jnp.where(rank < need, eqf, 0.0))
        slot = _dot(self_.astype(BF16), tri) + off_sel - 1.0
        pos_ref[0, :, j * LANES:(j + 1) * LANES] = jnp.where(self_ > 0.0, slot, -1.0)
        off_eq = off_eq + jnp.sum(eqf, axis=1, keepdims=True)
        off_sel = off_sel + jnp.sum(self_, axis=1, keepdims=True)


def _route(aff_t, cap):
    b, e, n = aff_t.shape
    tri = jnp.asarray(np.triu(np.ones((LANES, LANES), np.float32)), BF16)
    return pl.pallas_call(
        functools.partial(_route_kernel, cap=cap),
        grid=(b,),
        in_specs=[pl.BlockSpec((1, e, n), lambda bi: (bi, 0, 0)), pl.BlockSpec((LANES, LANES), lambda bi: (0, 0))],
        out_specs=pl.BlockSpec((1, e, n), lambda bi: (bi, 0, 0)),
        out_shape=jax.ShapeDtypeStruct((b, e, n), F32),
        compiler_params=_cparams("arbitrary"),
        name="route",
    )(aff_t, tri)


def _slot_index_kernel(pos_ref, aff_ref, idx_ref, gate_ref, *, cap, sc):
    pos = pos_ref[0, 0]
    aff = aff_ref[0, 0]
    n = pos.shape[1]
    for c in range(cap // sc):
        slot = (lax.broadcasted_iota(jnp.int32, (sc, LANES), 0) + c * sc).astype(F32)
        tok_acc = jnp.zeros((sc, LANES), F32)
        aff_acc = jnp.zeros((sc, LANES), F32)
        for j in range(n // LANES):
            hit = slot == pos[:, j * LANES:(j + 1) * LANES]
            tok = (lax.broadcasted_iota(jnp.int32, (1, LANES), 1) + j * LANES).astype(F32)
            tok_acc = tok_acc + jnp.where(hit, tok, 0.0)
            aff_acc = aff_acc + jnp.where(hit, aff[:, j * LANES:(j + 1) * LANES], 0.0)
        idx_ref[0, 0, :, c * sc:(c + 1) * sc] = jnp.sum(tok_acc.T, axis=0, keepdims=True).astype(jnp.int32)
        gate_ref[0, 0, :, c * sc:(c + 1) * sc] = jnp.sum(aff_acc.T, axis=0, keepdims=True)


def _slot_index(pos, aff_t, cap):
    b, e, n = pos.shape
    sc = min(LANES, cap)
    sel = lambda bi, ei: (bi, ei, 0, 0)
    return pl.pallas_call(
        functools.partial(_slot_index_kernel, cap=cap, sc=sc),
        grid=(b, e),
        in_specs=[pl.BlockSpec((1, 1, 1, n), sel), pl.BlockSpec((1, 1, 1, n), sel)],
        out_specs=[pl.BlockSpec((1, 1, 1, cap), sel), pl.BlockSpec((1, 1, 1, cap), sel)],
        out_shape=[jax.ShapeDtypeStruct((b, e, 1, cap), jnp.int32), jax.ShapeDtypeStruct((b, e, 1, cap), F32)],
        compiler_params=_cparams("arbitrary", "arbitrary"),
        name="slot_index",
    )(pos.reshape(b, e, 1, n), aff_t.reshape(b, e, 1, n))


def _gather_kernel(idx_ref, h3_ref, xe_ref, tile_ref, *, cap, nk, stride):
    for s in range(cap):
        t = idx_ref[0, 0, 0, s]
        tile_ref[pl.ds(s, nk, stride=stride), :] = h3_ref[0, pl.ds(pl.multiple_of(t * nk, nk), nk), :]
    for k in range(nk):
        xe_ref[0, 0, :, k * LANES:(k + 1) * LANES] = tile_ref[k * stride:k * stride + cap, :].astype(BF16)


def _gather(idx, h3, cap):
    b, e = idx.shape[:2]
    rows = h3.shape[1]
    nk = 8
    d = nk * LANES
    stride = cap + 8
    return pl.pallas_call(
        functools.partial(_gather_kernel, cap=cap, nk=nk, stride=stride),
        grid=(b, e),
        in_specs=[pl.BlockSpec((1, 1, 1, cap), lambda bi, ei: (bi, ei, 0, 0), memory_space=pltpu.SMEM),
                  pl.BlockSpec((1, rows, LANES), lambda bi, ei: (bi, 0, 0))],
        out_specs=pl.BlockSpec((1, 1, cap, d), lambda bi, ei: (ei, bi, 0, 0)),
        out_shape=jax.ShapeDtypeStruct((e, b, cap, d), BF16),
        scratch_shapes=[pltpu.VMEM((nk * stride, LANES), F32)],
        compiler_params=_cparams("arbitrary", "arbitrary"),
        name="moe_gather",
    )(idx, h3)


def _ffn_kernel(xe_ref, wg_ref, wu_ref, wd_ref, ye_ref, acc_ref, *, nf):
    f = pl.program_id(2)
    x = xe_ref[0]
    g = _dot(x, wg_ref[0, 0].astype(BF16))
    u = _dot(x, wu_ref[0, 0].astype(BF16))
    contrib = _dot((g * _sigmoid(g) * u).astype(BF16), wd_ref[0, 0].astype(BF16))

    @pl.when(f == 0)
    def _():
        acc_ref[...] = contrib

    @pl.when(f > 0)
    def _():
        acc_ref[...] += contrib

    @pl.when(f == nf - 1)
    def _():
        _store_token_major(ye_ref, acc_ref[...])


def _expert_ffn(xe, w_gate, w_up, w_down, layer):
    e, m, d = xe.shape
    dff = w_gate.shape[-1]
    nk = d // LANES
    tm = min(1024, m)
    tf = 512
    nf = dff // tf
    return pl.pallas_call(
        functools.partial(_ffn_kernel, nf=nf),
        grid=(e, m // tm, nf),
        in_specs=[pl.BlockSpec((1, tm, d), lambda ei, mi, f: (ei, mi, 0)),
                  pl.BlockSpec((1, 1, d, tf), lambda ei, mi, f: (layer, ei, 0, f)),
                  pl.BlockSpec((1, 1, d, tf), lambda ei, mi, f: (layer, ei, 0, f)),
                  pl.BlockSpec((1, 1, tf, d), lambda ei, mi, f: (layer, ei, f, 0))],
        out_specs=pl.BlockSpec((1, tm * nk, LANES), lambda ei, mi, f: (ei, mi, 0)),
        out_shape=jax.ShapeDtypeStruct((e, m * nk, LANES), F32),
        scratch_shapes=[pltpu.VMEM((tm, d), F32)],
        compiler_params=_cparams("arbitrary", "arbitrary", "arbitrary"),
        name="expert_ffn",
    )(xe, w_gate, w_up, w_down)


def _combine_kernel(idx_ref, gate_ref, ye_ref, x_ref, g2_ref, o_ref, acc_ref, *, cap, nk, tm, batch):
    k = pl.program_id(1)

    @pl.when(k == 0)
    def _():
        acc_ref[...] = jnp.zeros_like(acc_ref)

    @pl.when(k < N_EXPERTS)
    def _():
        for s0 in range(0, cap, batch):
            rows = []
            for s in range(s0, s0 + batch):
                start = pl.multiple_of(idx_ref[0, 0, 0, s] * nk, nk)
                rows.append((start, acc_ref[pl.ds(start, nk), :] + gate_ref[0, 0, 0, s] * ye_ref[0, 0, s * nk:(s + 1) * nk, :]))
            for start, val in rows:
                acc_ref[pl.ds(start, nk), :] = val

    @pl.when(k >= N_EXPERTS)
    def _():
        view = acc_ref.at[pl.ds(pl.multiple_of((k - N_EXPERTS) * (tm * nk), tm * nk), tm * nk), :]
        for c in range(nk):
            lanes = slice(c * LANES, (c + 1) * LANES)
            o_ref[0, :, lanes] = x_ref[0, :, lanes] + g2_ref[0, :, lanes] * view[pl.ds(c, tm, stride=nk), :]


def _combine(idx, gate, ye3, x, g2, cap):
    b, n, d = x.shape
    e = idx.shape[1]
    nk = d // LANES
    tm = min(512, n)
    nt = n // tm
    smem = lambda: pl.BlockSpec((1, 1, 1, cap), lambda bi, k: (bi, jnp.minimum(k, e - 1), 0, 0), memory_space=pltpu.SMEM)
    tile = lambda bi, k: (bi, jnp.maximum(k - e, 0), 0)
    return pl.pallas_call(
        functools.partial(_combine_kernel, cap=cap, nk=nk, tm=tm, batch=8),
        grid=(b, e + nt),
        in_specs=[smem(), smem(),
                  pl.BlockSpec((1, 1, cap * nk, LANES), lambda bi, k: (jnp.minimum(k, e - 1), bi, 0, 0)),
                  pl.BlockSpec((1, tm, d), tile),
                  pl.BlockSpec((1, 1, d), lambda bi, k: (bi, 0, 0))],
        out_specs=pl.BlockSpec((1, tm, d), tile),
        out_shape=jax.ShapeDtypeStruct((b, n, d), F32),
        scratch_shapes=[pltpu.VMEM((n * nk, LANES), F32)],
        compiler_params=_cparams("arbitrary", "arbitrary"),
        name="moe_combine",
    )(idx, gate, ye3, x, g2)


def _moe(x, h3, aff_t, g2, w_gate, w_up, w_down, layer):
    b, n, d = x.shape
    nk = d // LANES
    cap = CAPACITY_FACTOR * n // N_EXPERTS
    pos = _route(aff_t, cap)
    idx, gate = _slot_index(pos, aff_t, cap)
    xe = _gather(idx, h3, cap)
    ye3 = _expert_ffn(xe.reshape(N_EXPERTS, b * cap, d), w_gate, w_up, w_down, layer)
    return _combine(idx, gate, ye3.reshape(N_EXPERTS, b, cap * nk, LANES), x, g2, cap)


def _rope_tables(n_tok):
    rows = n_tok // GRID_W
    row = np.repeat(np.arange(rows), GRID_W).astype(np.float32)
    col = np.tile(np.arange(GRID_W), rows).astype(np.float32)
    n_freq = HEAD_DIM // 4
    inv = jnp.asarray(ROPE_THETA, F32) ** (-jnp.arange(n_freq, dtype=F32) / n_freq)
    ang_r = jnp.asarray(row)[:, None] * inv
    ang_c = jnp.asarray(col)[:, None] * inv
    cr, sr, cc, sc = jnp.cos(ang_r), jnp.sin(ang_r), jnp.cos(ang_c), jnp.sin(ang_c)
    zero = jnp.zeros_like(sr)
    cos = jnp.concatenate([cr, cr, cc, cc], axis=-1)
    sin_a = jnp.concatenate([-sr, zero, -sc, zero], axis=-1)
    sin_b = jnp.concatenate([zero, sr, zero, sc], axis=-1)
    return tuple(jnp.tile(t, (1, N_Q_HEADS)) for t in (cos, sin_a, sin_b))


def _head_mean_matrix():
    blk = np.kron(np.eye(N_Q_HEADS, dtype=np.float32), np.full((HEAD_DIM, HEAD_DIM), 1.0 / HEAD_DIM, np.float32))
    return jnp.asarray(blk, BF16)


def _rows(m, b):
    return m[:b, None, :]


def kernel(x, c, ctx, c_ctx, norm_mix_g, norm_ffn_g, w_mod, b_mod, ev_w_in, ev_conv_w, ev_conv_b, ev_ln_g, ev_ln_b, ev_q_norm_g, ev_k_norm_g, ev_w_out, od_w_in, od_conv_w, od_pool_w, od_pool_scale, od_w_out, w_router, w_gate, w_up, w_down):
    b, l, d = x.shape
    depth = w_mod.shape[0]
    last_even = ((depth - 1) // 2) * 2

    n_rows = -(-(b + 1) // 8) * 8
    c_rows = jnp.zeros((n_rows, d), F32).at[:b].set(c).at[b].set(c_ctx)
    mods = _modulations(c_rows, w_mod, b_mod)
    rope = _rope_tables(l)
    bd = _head_mean_matrix()

    for i in range(depth):
        j = i // 2
        is_even = i % 2 == 0
        ctx_live = i < last_even
        m6 = mods[i].reshape(n_rows, 6, d)
        sh1, sc1, g1, sh2, sc2, g2 = [_rows(m6[:, t], b) for t in range(6)]
        mc = [jnp.broadcast_to(m6[b, t][None, None, :], (b, 1, d)) for t in range(6)]
        sh1c, sc1c, g1c, sh2c, sc2c, g2c = mc
        nm_g = norm_mix_g[i][None, :]
        nf_g = norm_ffn_g[i][None, :]
        wr_hi, wr_lo = _split_bf16(jnp.pad(w_router[i], ((0, 0), (0, LANES - N_EXPERTS))))

        if is_even:
            w_in = ev_w_in[j].astype(BF16)
            wo = ev_w_out[j].astype(BF16)
            qg = jnp.tile(ev_q_norm_g[j], N_Q_HEADS)[None, :]
            kg = jnp.tile(ev_k_norm_g[j], N_KV_HEADS)[None, :]
            conv_args = (ev_conv_w[j], ev_conv_b[j][None, :], ev_ln_g[j][None, :], ev_ln_b[j][None, :])
            u, q, k, v = _even_in(x, sh1, sc1, nm_g, w_in, qg, kg, bd, rope)
            uc, qc, kc, vc = _even_in(ctx, sh1c, sc1c, nm_g, w_in, qg, kg, bd, None)
            attn = _attention(q, jnp.concatenate([kc, k], axis=1), jnp.concatenate([vc, v], axis=1))
            conv = _conf_conv(u, *conv_args)
            x, h2, aff = _even_out(conv, attn, wo, x, g1, nf_g, sh2, sc2, wr_hi, wr_lo)
            if ctx_live:
                attn_c = _attention(qc, kc, vc)
                conv_c = _conf_conv(uc, *conv_args)
                ctx, h2c, affc = _even_out(conv_c, attn_c, wo, ctx, g1c, nf_g, sh2c, sc2c, wr_hi, wr_lo)
        else:
            w_in = od_w_in[j].astype(BF16)
            wo = od_w_out[j].astype(BF16)
            mix_args = (od_conv_w[j], od_pool_w[j].astype(BF16), od_pool_scale[j][None, :], wo)
            z = _odd_in(x, sh1, sc1, nm_g, w_in)
            x, h2, aff = _odd_mix(z, *mix_args, x, g1, nf_g, sh2, sc2, wr_hi, wr_lo)
            if ctx_live:
                zc = _odd_in(ctx, sh1c, sc1c, nm_g, w_in)
                ctx, h2c, affc = _odd_mix(zc, *mix_args, ctx, g1c, nf_g, sh2c, sc2c, wr_hi, wr_lo)
        x = _moe(x, h2, aff, g2, w_gate, w_up, w_down, i)
        if ctx_live:
            ctx = _moe(ctx, h2c, affc, g2c, w_gate, w_up, w_down, i)
    return x
```

```python
import functools
import math

import jax
import jax.numpy as jnp
import numpy as np
from jax import lax
from jax.experimental import pallas as pl
from jax.experimental.pallas import tpu as pltpu

F32 = jnp.float32
BF16 = jnp.bfloat16

GRID_W = 64
N_Q_HEADS = 8
N_KV_HEADS = 2
HEAD_DIM = 64
D_ATTN = N_Q_HEADS * HEAD_DIM
D_KV = N_KV_HEADS * HEAD_DIM
ROPE_THETA = 10000.0
D_CONF = 512
D_SHORT = 512
D_POOL = 512
POOL_WINDOWS = (2, 4, 8, 16)
POOL_GROUP = D_POOL // len(POOL_WINDOWS)
N_EXPERTS = 16
CAPACITY_FACTOR = 2
EPS = 1e-6

LANES = 128
HALO = 16
VMEM_LIMIT = 56 * 1024 * 1024


def _sigmoid(x):
    return 1.0 / (1.0 + jnp.exp(-x))


def _dot(a, b):
    return jnp.dot(a, b, preferred_element_type=F32)


def _split_bf16(x):
    hi = x.astype(BF16)
    lo = (x - hi.astype(F32)).astype(BF16)
    return hi, lo


def _cparams(*sem):
    return pltpu.CompilerParams(dimension_semantics=sem, vmem_limit_bytes=VMEM_LIMIT)


def _mod_kernel(c_ref, w_ref, b_ref, o_ref):
    c = c_ref[...]
    s_hi, s_lo = _split_bf16(c * _sigmoid(c))
    w_hi, w_lo = _split_bf16(w_ref[0])
    o_ref[0] = _dot(s_hi, w_hi) + _dot(s_hi, w_lo) + _dot(s_lo, w_hi) + b_ref[0]


def _modulations(c_rows, w_mod, b_mod):
    depth, d, n = w_mod.shape
    r = c_rows.shape[0]
    tn = 1536
    return pl.pallas_call(
        _mod_kernel,
        grid=(depth, n // tn),
        in_specs=[pl.BlockSpec((r, d), lambda i, j: (0, 0)),
                  pl.BlockSpec((1, d, tn), lambda i, j: (i, 0, j)),
                  pl.BlockSpec((1, 1, tn), lambda i, j: (i, 0, j))],
        out_specs=pl.BlockSpec((1, r, tn), lambda i, j: (i, 0, j)),
        out_shape=jax.ShapeDtypeStruct((depth, r, n), F32),
        compiler_params=_cparams("arbitrary", "arbitrary"),
        name="modulations",
    )(c_rows, w_mod, b_mod.reshape(depth, 1, n))


def _modulated_norm(x, g, sh, sc):
    ms = jnp.mean(x * x, axis=-1, keepdims=True)
    return x * lax.rsqrt(ms + EPS) * g * (1.0 + sc) + sh


def _store_token_major(ref3, val):
    rows, d = val.shape
    nk = d // LANES
    for k in range(nk):
        ref3[0, pl.ds(k, rows, stride=nk), :] = val[:, k * LANES:(k + 1) * LANES]


def _post(x, y, g1, nf_g, sh2, sc2, wr_hi, wr_hilo, xo_ref, h3_ref, aff_ref):
    xn = x + g1 * y
    xo_ref[0] = xn
    h = _modulated_norm(xn, nf_g, sh2, sc2)
    h_hi, h_lo = _split_bf16(h)
    _store_token_major(h3_ref, h)
    both = _dot(h_hi, wr_hilo)
    lg = both[:, :LANES] + both[:, LANES:] + _dot(h_lo, wr_hi)
    lgt = lg.T[:N_EXPERTS]
    ex = jnp.exp(lgt - jnp.max(lgt, axis=0, keepdims=True))
    aff_ref[0] = ex / jnp.sum(ex, axis=0, keepdims=True)


def _head_rms(t, gain, bd):
    hi, lo = _split_bf16(t * t)
    ms = _dot(hi, bd) + _dot(lo, bd)
    return t * lax.rsqrt(ms + EPS) * gain


def _rope(t, cos, sin_a, sin_b):
    w = t.shape[-1]
    q = HEAD_DIM // 4
    return t * cos + pltpu.roll(t, w - q, 1) * sin_a + pltpu.roll(t, q, 1) * sin_b


def _even_in_kernel(*refs, rope):
    if rope:
        (x_ref, sh_ref, sc_ref, g_ref, w_ref, qg_ref, kg_ref, bd_ref, cos_ref, sa_ref, sb_ref,
         u_ref, q_ref, k_ref, v_ref) = refs
    else:
        x_ref, sh_ref, sc_ref, g_ref, w_ref, qg_ref, kg_ref, bd_ref, u_ref, q_ref, k_ref, v_ref = refs
    h = _modulated_norm(x_ref[0], g_ref[...], sh_ref[0], sc_ref[0])
    p = _dot(h.astype(BF16), w_ref[...])
    kv0 = 2 * D_CONF + D_ATTN
    u_ref[0] = p[:, :D_CONF] * _sigmoid(p[:, D_CONF:2 * D_CONF])
    q = _head_rms(p[:, 2 * D_CONF:kv0], qg_ref[...], bd_ref[...])
    k = _head_rms(p[:, kv0:kv0 + D_KV], kg_ref[...], bd_ref[:D_KV, :D_KV])
    if rope:
        cos, sa, sb = cos_ref[...], sa_ref[...], sb_ref[...]
        q = _rope(q, cos, sa, sb)
        k = _rope(k, cos[:, :D_KV], sa[:, :D_KV], sb[:, :D_KV])
    q_ref[0] = (q * (HEAD_DIM ** -0.5)).astype(BF16)
    k_ref[0] = k.astype(BF16)
    v_ref[0] = p[:, kv0 + D_KV:].astype(BF16)


def _even_in(x, sh, sc, g, w_bf, qg, kg, bd, rope_tabs):
    b, l, d = x.shape
    tm = min(512, l)
    n_in = w_bf.shape[1]
    row = lambda bi, t: (bi, 0, 0)
    tok = lambda bi, t: (bi, t, 0)
    const = lambda bi, t: (0, 0)
    in_specs = [pl.BlockSpec((1, tm, d), tok), pl.BlockSpec((1, 1, d), row), pl.BlockSpec((1, 1, d), row),
                pl.BlockSpec((1, d), const), pl.BlockSpec((d, n_in), const),
                pl.BlockSpec((1, D_ATTN), const), pl.BlockSpec((1, D_KV), const),
                pl.BlockSpec((D_ATTN, D_ATTN), const)]
    args = [x, sh, sc, g, w_bf, qg, kg, bd]
    if rope_tabs is not None:
        in_specs += [pl.BlockSpec((tm, D_ATTN), lambda bi, t: (t, 0))] * 3
        args += list(rope_tabs)
    return pl.pallas_call(
        functools.partial(_even_in_kernel, rope=rope_tabs is not None),
        grid=(b, l // tm),
        in_specs=in_specs,
        out_specs=[pl.BlockSpec((1, tm, D_CONF), tok), pl.BlockSpec((1, tm, D_ATTN), tok),
                   pl.BlockSpec((1, tm, D_KV), tok), pl.BlockSpec((1, tm, D_KV), tok)],
        out_shape=[jax.ShapeDtypeStruct((b, l, D_CONF), F32), jax.ShapeDtypeStruct((b, l, D_ATTN), BF16),
                   jax.ShapeDtypeStruct((b, l, D_KV), BF16), jax.ShapeDtypeStruct((b, l, D_KV), BF16)],
        compiler_params=_cparams("arbitrary", "arbitrary"),
        name="even_in_proj",
    )(*args)


def _fill_halo(xs_ref, prev_ref, cur_ref, next_ref, tm, nt):
    t = pl.program_id(1)
    xs_ref[0:HALO] = jnp.where(t > 0, prev_ref[0], 0.0)
    xs_ref[HALO:HALO + tm] = cur_ref[0]
    xs_ref[HALO + tm:HALO + tm + HALO] = jnp.where(t < nt - 1, next_ref[0], 0.0)


def _conf_conv_kernel(up_ref, uc_ref, un_ref, w_ref, b_ref, lg_ref, lb_ref, o_ref, xs_ref, sh_ref, *, tm, nt, rc):
    _fill_halo(xs_ref, up_ref, uc_ref, un_ref, tm, nt)
    width = w_ref.shape[0]
    pad = width // 2
    span = sh_ref.shape[1]
    for s in range(1, 8):
        sh_ref[s - 1] = xs_ref[pl.ds(s, span), :]
    for r in range(tm // rc):
        acc = jnp.zeros((rc, D_CONF), F32)
        for k in range(width):
            a, s = divmod(HALO - pad + k, 8)
            rows = pl.ds(r * rc + 8 * a, rc)
            win = xs_ref[rows, :] if s == 0 else sh_ref[s - 1, rows, :]
            acc = acc + win * w_ref[k:k + 1, :]
        u = acc + b_ref[...]
        mu = jnp.mean(u, axis=-1, keepdims=True)
        ctr = u - mu
        var = jnp.mean(ctr * ctr, axis=-1, keepdims=True)
        un = ctr * lax.rsqrt(var + EPS) * lg_ref[...] + lb_ref[...]
        o_ref[0, r * rc:(r + 1) * rc, :] = (un * _sigmoid(un)).astype(BF16)


def _halo_specs(tm, l, c):
    hb = tm // HALO
    last = l // HALO - 1
    return [pl.BlockSpec((1, HALO, c), lambda bi, t: (bi, jnp.maximum(t * hb - 1, 0), 0)),
            pl.BlockSpec((1, tm, c), lambda bi, t: (bi, t, 0)),
            pl.BlockSpec((1, HALO, c), lambda bi, t: (bi, jnp.minimum((t + 1) * hb, last), 0))]


def _conf_conv(u, conv_w, conv_b, ln_g, ln_b):
    b, l, c = u.shape
    tm = min(256, l)
    nt = l // tm
    const = lambda bi, t: (0, 0)
    return pl.pallas_call(
        functools.partial(_conf_conv_kernel, tm=tm, nt=nt, rc=32),
        grid=(b, nt),
        in_specs=_halo_specs(tm, l, c) + [pl.BlockSpec(conv_w.shape, const)] + [pl.BlockSpec((1, c), const)] * 3,
        out_specs=pl.BlockSpec((1, tm, c), lambda bi, t: (bi, t, 0)),
        out_shape=jax.ShapeDtypeStruct((b, l, c), BF16),
        scratch_shapes=[pltpu.VMEM((tm + 2 * HALO, c), F32), pltpu.VMEM((7, tm + 2 * HALO - 8, c), F32)],
        compiler_params=_cparams("arbitrary", "arbitrary"),
        name="conformer_conv",
    )(u, u, u, conv_w, conv_b, ln_g, ln_b)


def _attn_kernel(q_ref, kt_ref, v_ref, o_ref, *, chunks):
    q = q_ref[0]
    tq = q.shape[0]
    out = None
    for half in range(2):
        m = jnp.full((tq, 1), -jnp.inf, F32)
        l = jnp.zeros((tq, 1), F32)
        acc = jnp.zeros((tq, LANES), F32)
        for c0, cs in chunks:
            s = _dot(q, kt_ref[0, 0, half, :, c0:c0 + cs])
            m_new = jnp.maximum(m, jnp.max(s, axis=1, keepdims=True))
            alpha = jnp.exp(m - m_new)
            p = jnp.exp(s - m_new)
            l = alpha * l + jnp.sum(p, axis=1, keepdims=True)
            acc = alpha * acc + _dot(p.astype(BF16), v_ref[0, 0, half, c0:c0 + cs, :])
            m = m_new
        o = acc / l
        out = o if out is None else out + o
    o_ref[0] = out.astype(BF16)


def _kv_layouts(k, v):
    b, lk, _ = k.shape
    kt = k.reshape(b, lk, N_KV_HEADS, HEAD_DIM).transpose(0, 2, 3, 1)
    z = jnp.zeros_like(kt)
    kt = jnp.stack([jnp.concatenate([kt, z], axis=2), jnp.concatenate([z, kt], axis=2)], axis=2)
    vh = v.reshape(b, lk, N_KV_HEADS, HEAD_DIM).transpose(0, 2, 1, 3)
    zv = jnp.zeros_like(vh)
    vv = jnp.stack([jnp.concatenate([vh, zv], axis=3), jnp.concatenate([zv, vh], axis=3)], axis=2)
    return kt, vv


def _key_chunks(lk, size=1024):
    head = lk % size
    chunks = [(0, head)] if head else []
    return tuple(chunks + [(c, size) for c in range(head, lk, size)])


def _attention(q, k, v):
    b, l, _ = q.shape
    lk = k.shape[1]
    kt, vv = _kv_layouts(k, v)
    tq = min(1024, l)
    n_pairs = D_ATTN // LANES
    pairs_per_kv = n_pairs // N_KV_HEADS
    return pl.pallas_call(
        functools.partial(_attn_kernel, chunks=_key_chunks(lk)),
        grid=(b, n_pairs, l // tq),
        in_specs=[pl.BlockSpec((1, tq, LANES), lambda bi, j, t: (bi, t, j)),
                  pl.BlockSpec((1, 1, 2, LANES, lk), lambda bi, j, t: (bi, j // pairs_per_kv, 0, 0, 0)),
                  pl.BlockSpec((1, 1, 2, lk, LANES), lambda bi, j, t: (bi, j // pairs_per_kv, 0, 0, 0))],
        out_specs=pl.BlockSpec((1, tq, LANES), lambda bi, j, t: (bi, t, j)),
        out_shape=jax.ShapeDtypeStruct((b, l, D_ATTN), BF16),
        compiler_params=_cparams("arbitrary", "arbitrary", "arbitrary"),
        name="attention",
    )(q, kt, vv)


def _even_out_kernel(cv_ref, at_ref, wo_ref, x_ref, g1_ref, nfg_ref, sh2_ref, sc2_ref, wrh_ref, wrl_ref,
                     xo_ref, h2_ref, aff_ref):
    y = _dot(cv_ref[0], wo_ref[:D_CONF, :]) + _dot(at_ref[0], wo_ref[D_CONF:, :])
    _post(x_ref[0], y, g1_ref[0], nfg_ref[...], sh2_ref[0], sc2_ref[0], wrh_ref[...], wrl_ref[...],
          xo_ref, h2_ref, aff_ref)


def _post_specs(tm, d):
    row = lambda bi, t: (bi, 0, 0)
    const = lambda bi, t: (0, 0)
    in_specs = [pl.BlockSpec((1, tm, d), lambda bi, t: (bi, t, 0)), pl.BlockSpec((1, 1, d), row),
                pl.BlockSpec((1, d), const), pl.BlockSpec((1, 1, d), row), pl.BlockSpec((1, 1, d), row),
                pl.BlockSpec((d, LANES), const), pl.BlockSpec((d, 2 * LANES), const)]
    nk = d // LANES
    out_specs = [pl.BlockSpec((1, tm, d), lambda bi, t: (bi, t, 0)),
                 pl.BlockSpec((1, tm * nk, LANES), lambda bi, t: (bi, t, 0)),
                 pl.BlockSpec((1, N_EXPERTS, tm), lambda bi, t: (bi, 0, t))]
    return in_specs, out_specs


def _post_out_shapes(b, l, d):
    return [jax.ShapeDtypeStruct((b, l, d), F32), jax.ShapeDtypeStruct((b, l * (d // LANES), LANES), F32),
            jax.ShapeDtypeStruct((b, N_EXPERTS, l), F32)]


def _even_out(conv, attn, wo_bf, x, g1, nf_g, sh2, sc2, wr_hi, wr_lo):
    b, l, d = x.shape
    tm = min(512, l)
    tok = lambda bi, t: (bi, t, 0)
    post_in, post_out = _post_specs(tm, d)
    return pl.pallas_call(
        _even_out_kernel,
        grid=(b, l // tm),
        in_specs=[pl.BlockSpec((1, tm, D_CONF), tok), pl.BlockSpec((1, tm, D_ATTN), tok),
                  pl.BlockSpec(wo_bf.shape, lambda bi, t: (0, 0))] + post_in,
        out_specs=post_out,
        out_shape=_post_out_shapes(b, l, d),
        compiler_params=_cparams("arbitrary", "arbitrary"),
        name="even_out_proj",
    )(conv, attn, wo_bf, x, g1, nf_g, sh2, sc2, wr_hi, wr_lo)


def _odd_in_kernel(x_ref, sh_ref, sc_ref, g_ref, w_ref, z_ref):
    h = _modulated_norm(x_ref[0], g_ref[...], sh_ref[0], sc_ref[0])
    p = _dot(h.astype(BF16), w_ref[...])
    ds_ = D_SHORT
    z_ref[0, :, :ds_] = p[:, 2 * ds_:3 * ds_] * p[:, :ds_]
    z_ref[0, :, ds_:2 * ds_] = p[:, 3 * ds_:]
    z_ref[0, :, 2 * ds_:] = p[:, ds_:2 * ds_]


def _odd_in(x, sh, sc, g, w_bf):
    b, l, d = x.shape
    tm = min(512, l)
    row = lambda bi, t: (bi, 0, 0)
    tok = lambda bi, t: (bi, t, 0)
    const = lambda bi, t: (0, 0)
    nz = 2 * D_SHORT + D_POOL
    return pl.pallas_call(
        _odd_in_kernel,
        grid=(b, l // tm),
        in_specs=[pl.BlockSpec((1, tm, d), tok), pl.BlockSpec((1, 1, d), row), pl.BlockSpec((1, 1, d), row),
                  pl.BlockSpec((1, d), const), pl.BlockSpec(w_bf.shape, const)],
        out_specs=pl.BlockSpec((1, tm, nz), tok),
        out_shape=jax.ShapeDtypeStruct((b, l, nz), F32),
        compiler_params=_cparams("arbitrary", "arbitrary"),
        name="odd_in_proj",
    )(x, sh, sc, g, w_bf)


def _odd_mix_kernel(zp_ref, zc_ref, zn_ref, cw_ref, pw_ref, ps_ref, wo_ref,
                    x_ref, g1_ref, nfg_ref, sh2_ref, sc2_ref, wrh_ref, wrl_ref,
                    xo_ref, h2_ref, aff_ref, xs_ref, *, tm, nt, n_tok):
    _fill_halo(xs_ref, zp_ref, zc_ref, zn_ref, tm, nt)
    ds_ = D_SHORT
    width = cw_ref.shape[0]
    pad = width // 2
    conv = jnp.zeros((tm, ds_), F32)
    for k in range(width):
        conv = conv + xs_ref[pl.ds(HALO - pad + k, tm), 0:ds_] * cw_ref[k:k + 1, :]
    short = xs_ref[pl.ds(HALO, tm), 2 * ds_:3 * ds_] * conv
    pos = pl.program_id(1) * tm + lax.broadcasted_iota(jnp.int32, (tm, POOL_GROUP), 0)
    pooled = []
    for gi, w in enumerate(POOL_WINDOWS):
        c0 = ds_ + gi * POOL_GROUP
        tot = jnp.zeros((tm, POOL_GROUP), F32)
        for dlt in range(-(w // 2), w // 2):
            tot = tot + xs_ref[pl.ds(HALO + dlt, tm), c0:c0 + POOL_GROUP]
        cnt = (jnp.minimum(pos + w // 2, n_tok) - jnp.maximum(pos - w // 2, 0)).astype(F32)
        diff = tot / cnt - xs_ref[pl.ds(HALO, tm), c0:c0 + POOL_GROUP]
        pooled.append(_dot(diff.astype(BF16), pw_ref[gi]))
    pool = jnp.concatenate(pooled, axis=-1) * ps_ref[...]
    y = _dot(short.astype(BF16), wo_ref[:ds_, :]) + _dot(pool.astype(BF16), wo_ref[ds_:, :])
    _post(x_ref[0], y, g1_ref[0], nfg_ref[...], sh2_ref[0], sc2_ref[0], wrh_ref[...], wrl_ref[...],
          xo_ref, h2_ref, aff_ref)


def _odd_mix(z, conv_w, pool_w_bf, pool_scale, wo_bf, x, g1, nf_g, sh2, sc2, wr_hi, wr_lo):
    b, l, d = x.shape
    tm = min(256, l)
    nt = l // tm
    nz = z.shape[-1]
    const = lambda bi, t: (0, 0)
    post_in, post_out = _post_specs(tm, d)
    return pl.pallas_call(
        functools.partial(_odd_mix_kernel, tm=tm, nt=nt, n_tok=l),
        grid=(b, nt),
        in_specs=_halo_specs(tm, l, nz) + [pl.BlockSpec(conv_w.shape, const),
                                            pl.BlockSpec(pool_w_bf.shape, lambda bi, t: (0, 0, 0)),
                                            pl.BlockSpec((1, D_POOL), const),
                                            pl.BlockSpec(wo_bf.shape, const)] + post_in,
        out_specs=post_out,
        out_shape=_post_out_shapes(b, l, d),
        scratch_shapes=[pltpu.VMEM((tm + 2 * HALO, nz), F32)],
        compiler_params=_cparams("arbitrary", "arbitrary"),
        name="odd_mixer",
    )(z, z, z, conv_w, pool_w_bf, pool_scale, wo_bf, x, g1, nf_g, sh2, sc2, wr_hi, wr_lo)


def _route_kernel(aff_ref, tri_ref, pos_ref, *, cap):
    key = lax.bitcast_convert_type(aff_ref[0], jnp.int32)
    n = key.shape[1]
    capf = float(cap)
    thr = jnp.zeros((N_EXPERTS, 1), jnp.int32)
    for bit in range(30, -1, -1):
        cand = thr | (1 << bit)
        cnt = jnp.sum(jnp.where(key >= cand, 1.0, 0.0), axis=1, keepdims=True)
        thr = jnp.where(cnt >= capf, cand, thr)
    n_gt = jnp.sum(jnp.where(key > thr, 1.0, 0.0), axis=1, keepdims=True)
    need = capf - n_gt
    tri = tri_ref[...]
    off_eq = jnp.zeros((N_EXPERTS, 1), F32)
    off_sel = jnp.zeros((N_EXPERTS, 1), F32)
    for j in range(n // LANES):
        kj = key[:, j * LANES:(j + 1) * LANES]
        gt = kj > thr
        eqf = jnp.where(kj == thr, 1.0, 0.0)
        rank = _dot(eqf.astype(BF16), tri) + off_eq - eqf
        self_ = jnp.where(gt, 1.0, jnp.where(rank < need, eqf, 0.0))
        slot = _dot(self_.astype(BF16), tri) + off_sel - 1.0
        pos_ref[0, :, j * LANES:(j + 1) * LANES] = jnp.where(self_ > 0.0, slot, -1.0)
        off_eq = off_eq + jnp.sum(eqf, axis=1, keepdims=True)
        off_sel = off_sel + jnp.sum(self_, axis=1, keepdims=True)


def _route(aff_t, cap):
    b, e, n = aff_t.shape
    tri = jnp.asarray(np.triu(np.ones((LANES, LANES), np.float32)), BF16)
    return pl.pallas_call(
        functools.partial(_route_kernel, cap=cap),
        grid=(b,),
        in_specs=[pl.BlockSpec((1, e, n), lambda bi: (bi, 0, 0)), pl.BlockSpec((LANES, LANES), lambda bi: (0, 0))],
        out_specs=pl.BlockSpec((1, e, n), lambda bi: (bi, 0, 0)),
        out_shape=jax.ShapeDtypeStruct((b, e, n), F32),
        compiler_params=_cparams("arbitrary"),
        name="route",
    )(aff_t, tri)


def _slot_index_kernel(pos_ref, idx_ref, *, cap, sc):
    pos = pos_ref[0, 0]
    n = pos.shape[1]
    for c in range(cap // sc):
        slot = (lax.broadcasted_iota(jnp.int32, (sc, LANES), 0) + c * sc).astype(F32)
        tok_acc = jnp.zeros((sc, LANES), F32)
        for j in range(n // LANES):
            tok = (lax.broadcasted_iota(jnp.int32, (1, LANES), 1) + j * LANES).astype(F32)
            tok_acc = tok_acc + jnp.where(slot == pos[:, j * LANES:(j + 1) * LANES], tok, 0.0)
        idx_ref[0, 0, :, c * sc:(c + 1) * sc] = jnp.sum(tok_acc.T, axis=0, keepdims=True).astype(jnp.int32)


def _slot_index(pos, cap):
    b, e, n = pos.shape
    sc = min(LANES, cap)
    sel = lambda bi, ei: (bi, ei, 0, 0)
    return pl.pallas_call(
        functools.partial(_slot_index_kernel, cap=cap, sc=sc),
        grid=(b, e),
        in_specs=[pl.BlockSpec((1, 1, 1, n), sel)],
        out_specs=pl.BlockSpec((1, 1, 1, cap), sel),
        out_shape=jax.ShapeDtypeStruct((b, e, 1, cap), jnp.int32),
        compiler_params=_cparams("arbitrary", "arbitrary"),
        name="slot_index",
    )(pos.reshape(b, e, 1, n))


def _gather_kernel(idx_ref, h3_ref, xe_ref, tile_ref, *, cap, nk, stride):
    for s in range(cap):
        t = idx_ref[0, 0, 0, s]
        tile_ref[pl.ds(s, nk, stride=stride), :] = h3_ref[0, pl.ds(pl.multiple_of(t * nk, nk), nk), :]
    for k in range(nk):
        xe_ref[0, 0, :, k * LANES:(k + 1) * LANES] = tile_ref[k * stride:k * stride + cap, :].astype(BF16)


def _gather(idx, h3, cap):
    b, e = idx.shape[:2]
    rows = h3.shape[1]
    nk = 8
    d = nk * LANES
    stride = cap + 8
    return pl.pallas_call(
        functools.partial(_gather_kernel, cap=cap, nk=nk, stride=stride),
        grid=(b, e),
        in_specs=[pl.BlockSpec((1, 1, 1, cap), lambda bi, ei: (bi, ei, 0, 0), memory_space=pltpu.SMEM),
                  pl.BlockSpec((1, rows, LANES), lambda bi, ei: (bi, 0, 0))],
        out_specs=pl.BlockSpec((1, 1, cap, d), lambda bi, ei: (ei, bi, 0, 0)),
        out_shape=jax.ShapeDtypeStruct((e, b, cap, d), BF16),
        scratch_shapes=[pltpu.VMEM((nk * stride, LANES), F32)],
        compiler_params=_cparams("arbitrary", "arbitrary"),
        name="moe_gather",
    )(idx, h3)


def _ffn_kernel(xe_ref, wg_ref, wu_ref, wd_ref, ye_ref, acc_ref, *, nf):
    f = pl.program_id(2)
    x = xe_ref[0]
    g = _dot(x, wg_ref[0, 0].astype(BF16))
    u = _dot(x, wu_ref[0, 0].astype(BF16))

    @pl.when(f == 0)
    def _():
        acc_ref[...] = jnp.zeros_like(acc_ref)

    acc_ref[...] += _dot((g * _sigmoid(g) * u).astype(BF16), wd_ref[0, 0].astype(BF16))

    @pl.when(f == nf - 1)
    def _():
        _store_token_major(ye_ref, acc_ref[...])


def _expert_ffn(xe, w_gate, w_up, w_down, layer):
    e, m, d = xe.shape
    dff = w_gate.shape[-1]
    nk = d // LANES
    tm = min(1024, m)
    tf = 512
    nf = dff // tf
    return pl.pallas_call(
        functools.partial(_ffn_kernel, nf=nf),
        grid=(e, m // tm, nf),
        in_specs=[pl.BlockSpec((1, tm, d), lambda ei, mi, f: (ei, mi, 0)),
                  pl.BlockSpec((1, 1, d, tf), lambda ei, mi, f: (layer, ei, 0, f)),
                  pl.BlockSpec((1, 1, d, tf), lambda ei, mi, f: (layer, ei, 0, f)),
                  pl.BlockSpec((1, 1, tf, d), lambda ei, mi, f: (layer, ei, f, 0))],
        out_specs=pl.BlockSpec((1, tm * nk, LANES), lambda ei, mi, f: (ei, mi, 0)),
        out_shape=jax.ShapeDtypeStruct((e, m * nk, LANES), F32),
        scratch_shapes=[pltpu.VMEM((tm, d), F32)],
        compiler_params=_cparams("arbitrary", "arbitrary", "arbitrary"),
        name="expert_ffn",
    )(xe, w_gate, w_up, w_down)


def _combine_kernel(idx_ref, aff_ref, ye_ref, x_ref, g2_ref, o_ref, acc_ref, *, cap, nk, tm, batch):
    k = pl.program_id(1)

    @pl.when(k == 0)
    def _():
        acc_ref[...] = jnp.zeros_like(acc_ref)

    @pl.when(k < N_EXPERTS)
    def _():
        for s0 in range(0, cap, batch):
            rows = []
            for s in range(s0, s0 + batch):
                t = idx_ref[0, 0, 0, s]
                start = pl.multiple_of(t * nk, nk)
                rows.append((start, acc_ref[pl.ds(start, nk), :] + aff_ref[0, 0, 0, t] * ye_ref[0, 0, s * nk:(s + 1) * nk, :]))
            for start, val in rows:
                acc_ref[pl.ds(start, nk), :] = val

    @pl.when(k >= N_EXPERTS)
    def _():
        view = acc_ref.at[pl.ds(pl.multiple_of((k - N_EXPERTS) * (tm * nk), tm * nk), tm * nk), :]
        for c in range(nk):
            lanes = slice(c * LANES, (c + 1) * LANES)
            o_ref[0, :, lanes] = x_ref[0, :, lanes] + g2_ref[0, :, lanes] * view[pl.ds(c, tm, stride=nk), :]


def _combine(idx, aff_t, ye3, x, g2, cap):
    b, n, d = x.shape
    e = idx.shape[1]
    nk = d // LANES
    tm = min(512, n)
    nt = n // tm
    smem = lambda width: pl.BlockSpec((1, 1, 1, width), lambda bi, k: (bi, jnp.minimum(k, e - 1), 0, 0),
                                      memory_space=pltpu.SMEM)
    tile = lambda bi, k: (bi, jnp.maximum(k - e, 0), 0)
    return pl.pallas_call(
        functools.partial(_combine_kernel, cap=cap, nk=nk, tm=tm, batch=8),
        grid=(b, e + nt),
        in_specs=[smem(cap), smem(n),
                  pl.BlockSpec((1, 1, cap * nk, LANES), lambda bi, k: (jnp.minimum(k, e - 1), bi, 0, 0)),
                  pl.BlockSpec((1, tm, d), tile),
                  pl.BlockSpec((1, 1, d), lambda bi, k: (bi, 0, 0))],
        out_specs=pl.BlockSpec((1, tm, d), tile),
        out_shape=jax.ShapeDtypeStruct((b, n, d), F32),
        scratch_shapes=[pltpu.VMEM((n * nk, LANES), F32)],
        compiler_params=_cparams("arbitrary", "arbitrary"),
        name="moe_combine",
    )(idx, aff_t.reshape(b, e, 1, n), ye3, x, g2)


def _moe(x, h3, aff_t, g2, w_gate, w_up, w_down, layer):
    b, n, d = x.shape
    nk = d // LANES
    cap = CAPACITY_FACTOR * n // N_EXPERTS
    pos = _route(aff_t, cap)
    idx = _slot_index(pos, cap)
    xe = _gather(idx, h3, cap)
    ye3 = _expert_ffn(xe.reshape(N_EXPERTS, b * cap, d), w_gate, w_up, w_down, layer)
    return _combine(idx, aff_t, ye3.reshape(N_EXPERTS, b, cap * nk, LANES), x, g2, cap)


def _rope_tables(n_tok):
    rows = n_tok // GRID_W
    row = np.repeat(np.arange(rows), GRID_W).astype(np.float32)
    col = np.tile(np.arange(GRID_W), rows).astype(np.float32)
    n_freq = HEAD_DIM // 4
    inv = jnp.asarray(ROPE_THETA, F32) ** (-jnp.arange(n_freq, dtype=F32) / n_freq)
    ang_r = jnp.asarray(row)[:, None] * inv
    ang_c = jnp.asarray(col)[:, None] * inv
    cr, sr, cc, sc = jnp.cos(ang_r), jnp.sin(ang_r), jnp.cos(ang_c), jnp.sin(ang_c)
    zero = jnp.zeros_like(sr)
    cos = jnp.concatenate([cr, cr, cc, cc], axis=-1)
    sin_a = jnp.concatenate([-sr, zero, -sc, zero], axis=-1)
    sin_b = jnp.concatenate([zero, sr, zero, sc], axis=-1)
    return tuple(jnp.tile(t, (1, N_Q_HEADS)) for t in (cos, sin_a, sin_b))


def _head_mean_matrix():
    blk = np.kron(np.eye(N_Q_HEADS, dtype=np.float32), np.full((HEAD_DIM, HEAD_DIM), 1.0 / HEAD_DIM, np.float32))
    return jnp.asarray(blk, BF16)


def _rows(m, b):
    return m[:b, None, :]


def kernel(x, c, ctx, c_ctx, norm_mix_g, norm_ffn_g, w_mod, b_mod, ev_w_in, ev_conv_w, ev_conv_b, ev_ln_g, ev_ln_b, ev_q_norm_g, ev_k_norm_g, ev_w_out, od_w_in, od_conv_w, od_pool_w, od_pool_scale, od_w_out, w_router, w_gate, w_up, w_down):
    b, l, d = x.shape
    depth = w_mod.shape[0]
    last_even = ((depth - 1) // 2) * 2

    n_rows = -(-(b + 1) // 8) * 8
    c_rows = jnp.zeros((n_rows, d), F32).at[:b].set(c).at[b].set(c_ctx)
    mods = _modulations(c_rows, w_mod, b_mod)
    rope = _rope_tables(l)
    bd = _head_mean_matrix()

    for i in range(depth):
        j = i // 2
        is_even = i % 2 == 0
        ctx_live = i < last_even
        m6 = mods[i].reshape(n_rows, 6, d)
        sh1, sc1, g1, sh2, sc2, g2 = [_rows(m6[:, t], b) for t in range(6)]
        mc = [jnp.broadcast_to(m6[b, t][None, None, :], (b, 1, d)) for t in range(6)]
        sh1c, sc1c, g1c, sh2c, sc2c, g2c = mc
        nm_g = norm_mix_g[i][None, :]
        nf_g = norm_ffn_g[i][None, :]
        wr_hi, wr_lo = _split_bf16(jnp.pad(w_router[i], ((0, 0), (0, LANES - N_EXPERTS))))
        wr_lo = jnp.concatenate([wr_hi, wr_lo], axis=1)

        if is_even:
            w_in = ev_w_in[j].astype(BF16)
            wo = ev_w_out[j].astype(BF16)
            qg = jnp.tile(ev_q_norm_g[j], N_Q_HEADS)[None, :]
            kg = jnp.tile(ev_k_norm_g[j], N_KV_HEADS)[None, :]
            conv_args = (ev_conv_w[j], ev_conv_b[j][None, :], ev_ln_g[j][None, :], ev_ln_b[j][None, :])
            u, q, k, v = _even_in(x, sh1, sc1, nm_g, w_in, qg, kg, bd, rope)
            uc, qc, kc, vc = _even_in(ctx, sh1c, sc1c, nm_g, w_in, qg, kg, bd, None)
            attn = _attention(q, jnp.concatenate([kc, k], axis=1), jnp.concatenate([vc, v], axis=1))
            conv = _conf_conv(u, *conv_args)
            x, h2, aff = _even_out(conv, attn, wo, x, g1, nf_g, sh2, sc2, wr_hi, wr_lo)
            if ctx_live:
                attn_c = _attention(qc, kc, vc)
                conv_c = _conf_conv(uc, *conv_args)
                ctx, h2c, affc = _even_out(conv_c, attn_c, wo, ctx, g1c, nf_g, sh2c, sc2c, wr_hi, wr_lo)
        else:
            w_in = od_w_in[j].astype(BF16)
            wo = od_w_out[j].astype(BF16)
            mix_args = (od_conv_w[j], od_pool_w[j].astype(BF16), od_pool_scale[j][None, :], wo)
            z = _odd_in(x, sh1, sc1, nm_g, w_in)
            x, h2, aff = _odd_mix(z, *mix_args, x, g1, nf_g, sh2, sc2, wr_hi, wr_lo)
            if ctx_live:
                zc = _odd_in(ctx, sh1c, sc1c, nm_g, w_in)
                ctx, h2c, affc = _odd_mix(zc, *mix_args, ctx, g1c, nf_g, sh2c, sc2c, wr_hi, wr_lo)
        x = _moe(x, h2, aff, g2, w_gate, w_up, w_down, i)
        if ctx_live:
            ctx = _moe(ctx, h2c, affc, g2c, w_gate, w_up, w_down, i)
    return x
```

```python
import functools
import math

import jax
import jax.numpy as jnp
import numpy as np
from jax import lax
from jax.experimental import pallas as pl
from jax.experimental.pallas import tpu as pltpu

F32 = jnp.float32
BF16 = jnp.bfloat16

GRID_W = 64
N_Q_HEADS = 8
N_KV_HEADS = 2
HEAD_DIM = 64
D_ATTN = N_Q_HEADS * HEAD_DIM
D_KV = N_KV_HEADS * HEAD_DIM
ROPE_THETA = 10000.0
D_CONF = 512
D_SHORT = 512
D_POOL = 512
POOL_WINDOWS = (2, 4, 8, 16)
POOL_GROUP = D_POOL // len(POOL_WINDOWS)
N_EXPERTS = 16
CAPACITY_FACTOR = 2
EPS = 1e-6

LANES = 128
HALO = 16
VMEM_LIMIT = 56 * 1024 * 1024


def _sigmoid(x):
    return 1.0 / (1.0 + jnp.exp(-x))


def _dot(a, b):
    return jnp.dot(a, b, preferred_element_type=F32)


def _split_bf16(x):
    hi = x.astype(BF16)
    lo = (x - hi.astype(F32)).astype(BF16)
    return hi, lo


def _cparams(*sem):
    return pltpu.CompilerParams(dimension_semantics=sem, vmem_limit_bytes=VMEM_LIMIT)


def _mod_kernel(c_ref, w_ref, b_ref, o_ref):
    c = c_ref[...]
    s_hi, s_lo = _split_bf16(c * _sigmoid(c))
    w_hi, w_lo = _split_bf16(w_ref[0])
    o_ref[0] = _dot(s_hi, w_hi) + _dot(s_hi, w_lo) + _dot(s_lo, w_hi) + b_ref[0]


def _modulations(c_rows, w_mod, b_mod):
    depth, d, n = w_mod.shape
    r = c_rows.shape[0]
    tn = 1536
    return pl.pallas_call(
        _mod_kernel,
        grid=(depth, n // tn),
        in_specs=[pl.BlockSpec((r, d), lambda i, j: (0, 0)),
                  pl.BlockSpec((1, d, tn), lambda i, j: (i, 0, j)),
                  pl.BlockSpec((1, 1, tn), lambda i, j: (i, 0, j))],
        out_specs=pl.BlockSpec((1, r, tn), lambda i, j: (i, 0, j)),
        out_shape=jax.ShapeDtypeStruct((depth, r, n), F32),
        compiler_params=_cparams("arbitrary", "arbitrary"),
        name="modulations",
    )(c_rows, w_mod, b_mod.reshape(depth, 1, n))


def _modulated_norm(x, g, sh, sc):
    ms = jnp.mean(x * x, axis=-1, keepdims=True)
    return x * lax.rsqrt(ms + EPS) * g * (1.0 + sc) + sh


def _store_token_major(ref3, val):
    rows, d = val.shape
    nk = d // LANES
    for k in range(nk):
        ref3[0, pl.ds(k, rows, stride=nk), :] = val[:, k * LANES:(k + 1) * LANES]


def _post(x, y, g1, nf_g, sh2, sc2, wr_hi, wr_hilo, xo_ref, h3_ref, aff_ref):
    xn = x + g1 * y
    xo_ref[0] = xn
    h = _modulated_norm(xn, nf_g, sh2, sc2)
    h_hi, h_lo = _split_bf16(h)
    _store_token_major(h3_ref, h)
    both = _dot(h_hi, wr_hilo)
    lg = both[:, :LANES] + both[:, LANES:] + _dot(h_lo, wr_hi)
    lgt = lg.T[:N_EXPERTS]
    ex = jnp.exp(lgt - jnp.max(lgt, axis=0, keepdims=True))
    aff_ref[0] = ex / jnp.sum(ex, axis=0, keepdims=True)


def _head_rms(t, gain, bd):
    hi, lo = _split_bf16(t * t)
    ms = _dot(hi, bd) + _dot(lo, bd)
    return t * lax.rsqrt(ms + EPS) * gain


def _rope(t, cos, sin_a, sin_b):
    w = t.shape[-1]
    q = HEAD_DIM // 4
    return t * cos + pltpu.roll(t, w - q, 1) * sin_a + pltpu.roll(t, q, 1) * sin_b


def _even_in_kernel(*refs, rope):
    if rope:
        (x_ref, sh_ref, sc_ref, g_ref, w_ref, qg_ref, kg_ref, bd_ref, cos_ref, sa_ref, sb_ref,
         u_ref, q_ref, k_ref, v_ref) = refs
    else:
        x_ref, sh_ref, sc_ref, g_ref, w_ref, qg_ref, kg_ref, bd_ref, u_ref, q_ref, k_ref, v_ref = refs
    h = _modulated_norm(x_ref[0], g_ref[...], sh_ref[0], sc_ref[0])
    p = _dot(h.astype(BF16), w_ref[...])
    kv0 = 2 * D_CONF + D_ATTN
    u_ref[0] = p[:, :D_CONF] * _sigmoid(p[:, D_CONF:2 * D_CONF])
    q = _head_rms(p[:, 2 * D_CONF:kv0], qg_ref[...], bd_ref[...])
    k = _head_rms(p[:, kv0:kv0 + D_KV], kg_ref[...], bd_ref[:D_KV, :D_KV])
    if rope:
        cos, sa, sb = cos_ref[...], sa_ref[...], sb_ref[...]
        q = _rope(q, cos, sa, sb)
        k = _rope(k, cos[:, :D_KV], sa[:, :D_KV], sb[:, :D_KV])
    q_ref[0] = (q * (HEAD_DIM ** -0.5)).astype(BF16)
    k_ref[0] = k.astype(BF16)
    v_ref[0] = p[:, kv0 + D_KV:].astype(BF16)


def _even_in(x, sh, sc, g, w_bf, qg, kg, bd, rope_tabs):
    b, l, d = x.shape
    tm = min(512, l)
    n_in = w_bf.shape[1]
    row = lambda bi, t: (bi, 0, 0)
    tok = lambda bi, t: (bi, t, 0)
    const = lambda bi, t: (0, 0)
    in_specs = [pl.BlockSpec((1, tm, d), tok), pl.BlockSpec((1, 1, d), row), pl.BlockSpec((1, 1, d), row),
                pl.BlockSpec((1, d), const), pl.BlockSpec((d, n_in), const),
                pl.BlockSpec((1, D_ATTN), const), pl.BlockSpec((1, D_KV), const),
                pl.BlockSpec((D_ATTN, D_ATTN), const)]
    args = [x, sh, sc, g, w_bf, qg, kg, bd]
    if rope_tabs is not None:
        in_specs += [pl.BlockSpec((tm, D_ATTN), lambda bi, t: (t, 0))] * 3
        args += list(rope_tabs)
    return pl.pallas_call(
        functools.partial(_even_in_kernel, rope=rope_tabs is not None),
        grid=(b, l // tm),
        in_specs=in_specs,
        out_specs=[pl.BlockSpec((1, tm, D_CONF), tok), pl.BlockSpec((1, tm, D_ATTN), tok),
                   pl.BlockSpec((1, tm, D_KV), tok), pl.BlockSpec((1, tm, D_KV), tok)],
        out_shape=[jax.ShapeDtypeStruct((b, l, D_CONF), F32), jax.ShapeDtypeStruct((b, l, D_ATTN), BF16),
                   jax.ShapeDtypeStruct((b, l, D_KV), BF16), jax.ShapeDtypeStruct((b, l, D_KV), BF16)],
        compiler_params=_cparams("arbitrary", "arbitrary"),
        name="even_in_proj",
    )(*args)


def _fill_halo(xs_ref, prev_ref, cur_ref, next_ref, tm, nt):
    t = pl.program_id(1)
    xs_ref[0:HALO] = jnp.where(t > 0, prev_ref[0], 0.0)
    xs_ref[HALO:HALO + tm] = cur_ref[0]
    xs_ref[HALO + tm:HALO + tm + HALO] = jnp.where(t < nt - 1, next_ref[0], 0.0)


def _conf_conv_kernel(up_ref, uc_ref, un_ref, w_ref, b_ref, lg_ref, lb_ref, o_ref, xs_ref, sh_ref, *, tm, nt, rc):
    _fill_halo(xs_ref, up_ref, uc_ref, un_ref, tm, nt)
    width = w_ref.shape[0]
    pad = width // 2
    span = sh_ref.shape[1]
    for s in range(1, 8):
        sh_ref[s - 1] = xs_ref[pl.ds(s, span), :]
    for r in range(tm // rc):
        acc = jnp.zeros((rc, D_CONF), F32)
        for k in range(width):
            a, s = divmod(HALO - pad + k, 8)
            rows = pl.ds(r * rc + 8 * a, rc)
            win = xs_ref[rows, :] if s == 0 else sh_ref[s - 1, rows, :]
            acc = acc + win * w_ref[k:k + 1, :]
        u = acc + b_ref[...]
        mu = jnp.mean(u, axis=-1, keepdims=True)
        ctr = u - mu
        var = jnp.mean(ctr * ctr, axis=-1, keepdims=True)
        un = ctr * lax.rsqrt(var + EPS) * lg_ref[...] + lb_ref[...]
        o_ref[0, r * rc:(r + 1) * rc, :] = (un * _sigmoid(un)).astype(BF16)


def _halo_specs(tm, l, c):
    hb = tm // HALO
    last = l // HALO - 1
    return [pl.BlockSpec((1, HALO, c), lambda bi, t: (bi, jnp.maximum(t * hb - 1, 0), 0)),
            pl.BlockSpec((1, tm, c), lambda bi, t: (bi, t, 0)),
            pl.BlockSpec((1, HALO, c), lambda bi, t: (bi, jnp.minimum((t + 1) * hb, last), 0))]


def _conf_conv(u, conv_w, conv_b, ln_g, ln_b):
    b, l, c = u.shape
    tm = min(256, l)
    nt = l // tm
    const = lambda bi, t: (0, 0)
    return pl.pallas_call(
        functools.partial(_conf_conv_kernel, tm=tm, nt=nt, rc=32),
        grid=(b, nt),
        in_specs=_halo_specs(tm, l, c) + [pl.BlockSpec(conv_w.shape, const)] + [pl.BlockSpec((1, c), const)] * 3,
        out_specs=pl.BlockSpec((1, tm, c), lambda bi, t: (bi, t, 0)),
        out_shape=jax.ShapeDtypeStruct((b, l, c), BF16),
        scratch_shapes=[pltpu.VMEM((tm + 2 * HALO, c), F32), pltpu.VMEM((7, tm + 2 * HALO - 8, c), F32)],
        compiler_params=_cparams("arbitrary", "arbitrary"),
        name="conformer_conv",
    )(u, u, u, conv_w, conv_b, ln_g, ln_b)


def _attn_kernel(q_ref, kt_ref, v_ref, o_ref, *, chunks):
    q = q_ref[0]
    tq = q.shape[0]
    halves = []
    for half in range(2):
        m = jnp.full((tq, 1), -jnp.inf, F32)
        acc = jnp.zeros((tq, LANES), F32)
        for c0, cs in chunks:
            s = _dot(q, kt_ref[0, 0, half, :, c0:c0 + cs])
            m_new = jnp.maximum(m, jnp.max(s, axis=1, keepdims=True))
            p = jnp.exp(s - m_new)
            acc = jnp.exp(m - m_new) * acc + _dot(p.astype(BF16), v_ref[0, 0, half, c0:c0 + cs, :])
            m = m_new
        den = (1 - half) * HEAD_DIM
        halves.append(acc / acc[:, den:den + 1])
    lane = lax.broadcasted_iota(jnp.int32, (tq, LANES), 1)
    o_ref[0] = jnp.where(lane < HEAD_DIM, halves[0], halves[1]).astype(BF16)


def _kv_layouts(k, v):
    b, lk, _ = k.shape
    kt = k.reshape(b, lk, N_KV_HEADS, HEAD_DIM).transpose(0, 2, 3, 1)
    z = jnp.zeros_like(kt)
    kt = jnp.stack([jnp.concatenate([kt, z], axis=2), jnp.concatenate([z, kt], axis=2)], axis=2)
    vh = v.reshape(b, lk, N_KV_HEADS, HEAD_DIM).transpose(0, 2, 1, 3)
    pad = jnp.zeros_like(vh).at[..., 0].set(1.0)
    vv = jnp.stack([jnp.concatenate([vh, pad], axis=3), jnp.concatenate([pad, vh], axis=3)], axis=2)
    return kt, vv


def _key_chunks(lk, size=1024):
    head = lk % size
    chunks = [(0, head)] if head else []
    return tuple(chunks + [(c, size) for c in range(head, lk, size)])


def _attention(q, k, v):
    b, l, _ = q.shape
    lk = k.shape[1]
    kt, vv = _kv_layouts(k, v)
    tq = min(1024, l)
    n_pairs = D_ATTN // LANES
    pairs_per_kv = n_pairs // N_KV_HEADS
    return pl.pallas_call(
        functools.partial(_attn_kernel, chunks=_key_chunks(lk)),
        grid=(b, n_pairs, l // tq),
        in_specs=[pl.BlockSpec((1, tq, LANES), lambda bi, j, t: (bi, t, j)),
                  pl.BlockSpec((1, 1, 2, LANES, lk), lambda bi, j, t: (bi, j // pairs_per_kv, 0, 0, 0)),
                  pl.BlockSpec((1, 1, 2, lk, LANES), lambda bi, j, t: (bi, j // pairs_per_kv, 0, 0, 0))],
        out_specs=pl.BlockSpec((1, tq, LANES), lambda bi, j, t: (bi, t, j)),
        out_shape=jax.ShapeDtypeStruct((b, l, D_ATTN), BF16),
        compiler_params=_cparams("arbitrary", "arbitrary", "arbitrary"),
        name="attention",
    )(q, kt, vv)


def _even_out_kernel(cv_ref, at_ref, wo_ref, x_ref, g1_ref, nfg_ref, sh2_ref, sc2_ref, wrh_ref, wrl_ref,
                     xo_ref, h2_ref, aff_ref):
    y = _dot(cv_ref[0], wo_ref[:D_CONF, :]) + _dot(at_ref[0], wo_ref[D_CONF:, :])
    _post(x_ref[0], y, g1_ref[0], nfg_ref[...], sh2_ref[0], sc2_ref[0], wrh_ref[...], wrl_ref[...],
          xo_ref, h2_ref, aff_ref)


def _post_specs(tm, d):
    row = lambda bi, t: (bi, 0, 0)
    const = lambda bi, t: (0, 0)
    in_specs = [pl.BlockSpec((1, tm, d), lambda bi, t: (bi, t, 0)), pl.BlockSpec((1, 1, d), row),
                pl.BlockSpec((1, d), const), pl.BlockSpec((1, 1, d), row), pl.BlockSpec((1, 1, d), row),
                pl.BlockSpec((d, LANES), const), pl.BlockSpec((d, 2 * LANES), const)]
    nk = d // LANES
    out_specs = [pl.BlockSpec((1, tm, d), lambda bi, t: (bi, t, 0)),
                 pl.BlockSpec((1, tm * nk, LANES), lambda bi, t: (bi, t, 0)),
                 pl.BlockSpec((1, N_EXPERTS, tm), lambda bi, t: (bi, 0, t))]
    return in_specs, out_specs


def _post_out_shapes(b, l, d):
    return [jax.ShapeDtypeStruct((b, l, d), F32), jax.ShapeDtypeStruct((b, l * (d // LANES), LANES), F32),
            jax.ShapeDtypeStruct((b, N_EXPERTS, l), F32)]


def _even_out(conv, attn, wo_bf, x, g1, nf_g, sh2, sc2, wr_hi, wr_lo):
    b, l, d = x.shape
    tm = min(512, l)
    tok = lambda bi, t: (bi, t, 0)
    post_in, post_out = _post_specs(tm, d)
    return pl.pallas_call(
        _even_out_kernel,
        grid=(b, l // tm),
        in_specs=[pl.BlockSpec((1, tm, D_CONF), tok), pl.BlockSpec((1, tm, D_ATTN), tok),
                  pl.BlockSpec(wo_bf.shape, lambda bi, t: (0, 0))] + post_in,
        out_specs=post_out,
        out_shape=_post_out_shapes(b, l, d),
        compiler_params=_cparams("arbitrary", "arbitrary"),
        name="even_out_proj",
    )(conv, attn, wo_bf, x, g1, nf_g, sh2, sc2, wr_hi, wr_lo)


def _odd_in_kernel(x_ref, sh_ref, sc_ref, g_ref, w_ref, z_ref):
    h = _modulated_norm(x_ref[0], g_ref[...], sh_ref[0], sc_ref[0])
    p = _dot(h.astype(BF16), w_ref[...])
    ds_ = D_SHORT
    z_ref[0, :, :ds_] = p[:, 2 * ds_:3 * ds_] * p[:, :ds_]
    z_ref[0, :, ds_:2 * ds_] = p[:, 3 * ds_:]
    z_ref[0, :, 2 * ds_:] = p[:, ds_:2 * ds_]


def _odd_in(x, sh, sc, g, w_bf):
    b, l, d = x.shape
    tm = min(512, l)
    row = lambda bi, t: (bi, 0, 0)
    tok = lambda bi, t: (bi, t, 0)
    const = lambda bi, t: (0, 0)
    nz = 2 * D_SHORT + D_POOL
    return pl.pallas_call(
        _odd_in_kernel,
        grid=(b, l // tm),
        in_specs=[pl.BlockSpec((1, tm, d), tok), pl.BlockSpec((1, 1, d), row), pl.BlockSpec((1, 1, d), row),
                  pl.BlockSpec((1, d), const), pl.BlockSpec(w_bf.shape, const)],
        out_specs=pl.BlockSpec((1, tm, nz), tok),
        out_shape=jax.ShapeDtypeStruct((b, l, nz), F32),
        compiler_params=_cparams("arbitrary", "arbitrary"),
        name="odd_in_proj",
    )(x, sh, sc, g, w_bf)


def _odd_mix_kernel(zp_ref, zc_ref, zn_ref, cw_ref, pw_ref, ps_ref, wo_ref,
                    x_ref, g1_ref, nfg_ref, sh2_ref, sc2_ref, wrh_ref, wrl_ref,
                    xo_ref, h2_ref, aff_ref, xs_ref, *, tm, nt, n_tok):
    _fill_halo(xs_ref, zp_ref, zc_ref, zn_ref, tm, nt)
    ds_ = D_SHORT
    width = cw_ref.shape[0]
    pad = width // 2
    conv = jnp.zeros((tm, ds_), F32)
    for k in range(width):
        conv = conv + xs_ref[pl.ds(HALO - pad + k, tm), 0:ds_] * cw_ref[k:k + 1, :]
    short = xs_ref[pl.ds(HALO, tm), 2 * ds_:3 * ds_] * conv
    pos = pl.program_id(1) * tm + lax.broadcasted_iota(jnp.int32, (tm, POOL_GROUP), 0)
    pooled = []
    for gi, w in enumerate(POOL_WINDOWS):
        c0 = ds_ + gi * POOL_GROUP
        tot = jnp.zeros((tm, POOL_GROUP), F32)
        for dlt in range(-(w // 2), w // 2):
            tot = tot + xs_ref[pl.ds(HALO + dlt, tm), c0:c0 + POOL_GROUP]
        cnt = (jnp.minimum(pos + w // 2, n_tok) - jnp.maximum(pos - w // 2, 0)).astype(F32)
        diff = tot / cnt - xs_ref[pl.ds(HALO, tm), c0:c0 + POOL_GROUP]
        pooled.append(_dot(diff.astype(BF16), pw_ref[gi]))
    pool = jnp.concatenate(pooled, axis=-1) * ps_ref[...]
    y = _dot(short.astype(BF16), wo_ref[:ds_, :]) + _dot(pool.astype(BF16), wo_ref[ds_:, :])
    _post(x_ref[0], y, g1_ref[0], nfg_ref[...], sh2_ref[0], sc2_ref[0], wrh_ref[...], wrl_ref[...],
          xo_ref, h2_ref, aff_ref)


def _odd_mix(z, conv_w, pool_w_bf, pool_scale, wo_bf, x, g1, nf_g, sh2, sc2, wr_hi, wr_lo):
    b, l, d = x.shape
    tm = min(256, l)
    nt = l // tm
    nz = z.shape[-1]
    const = lambda bi, t: (0, 0)
    post_in, post_out = _post_specs(tm, d)
    return pl.pallas_call(
        functools.partial(_odd_mix_kernel, tm=tm, nt=nt, n_tok=l),
        grid=(b, nt),
        in_specs=_halo_specs(tm, l, nz) + [pl.BlockSpec(conv_w.shape, const),
                                            pl.BlockSpec(pool_w_bf.shape, lambda bi, t: (0, 0, 0)),
                                            pl.BlockSpec((1, D_POOL), const),
                                            pl.BlockSpec(wo_bf.shape, const)] + post_in,
        out_specs=post_out,
        out_shape=_post_out_shapes(b, l, d),
        scratch_shapes=[pltpu.VMEM((tm + 2 * HALO, nz), F32)],
        compiler_params=_cparams("arbitrary", "arbitrary"),
        name="odd_mixer",
    )(z, z, z, conv_w, pool_w_bf, pool_scale, wo_bf, x, g1, nf_g, sh2, sc2, wr_hi, wr_lo)


def _route_kernel(aff_ref, tri_ref, pos_ref, *, cap):
    key = lax.bitcast_convert_type(aff_ref[0], jnp.int32)
    n = key.shape[1]
    capf = float(cap)
    thr = jnp.zeros((N_EXPERTS, 1), jnp.int32)
    for bit in range(30, -1, -1):
        cand = thr | (1 << bit)
        cnt = jnp.sum(jnp.where(key >= cand, 1.0, 0.0), axis=1, keepdims=True)
        thr = jnp.where(cnt >= capf, cand, thr)
    n_gt = jnp.sum(jnp.where(key > thr, 1.0, 0.0), axis=1, keepdims=True)
    need = capf - n_gt
    tri = tri_ref[...]
    off_eq = jnp.zeros((N_EXPERTS, 1), F32)
    off_sel = jnp.zeros((N_EXPERTS, 1), F32)
    for j in range(n // LANES):
        kj = key[:, j * LANES:(j + 1) * LANES]
        gt = kj > thr
        eqf = jnp.where(kj == thr, 1.0, 0.0)
        rank = _dot(eqf.astype(BF16), tri) + off_eq - eqf
        self_ = jnp.where(gt, 1.0, jnp.where(rank < need, eqf, 0.0))
        slot = _dot(self_.astype(BF16), tri) + off_sel - 1.0
        pos_ref[0, :, j * LANES:(j + 1) * LANES] = jnp.where(self_ > 0.0, slot, -1.0)
        off_eq = off_eq + jnp.sum(eqf, axis=1, keepdims=True)
        off_sel = off_sel + jnp.sum(self_, axis=1, keepdims=True)


def _route(aff_t, cap):
    b, e, n = aff_t.shape
    tri = jnp.asarray(np.triu(np.ones((LANES, LANES), np.float32)), BF16)
    return pl.pallas_call(
        functools.partial(_route_kernel, cap=cap),
        grid=(b,),
        in_specs=[pl.BlockSpec((1, e, n), lambda bi: (bi, 0, 0)), pl.BlockSpec((LANES, LANES), lambda bi: (0, 0))],
        out_specs=pl.BlockSpec((1, e, n), lambda bi: (bi, 0, 0)),
        out_shape=jax.ShapeDtypeStruct((b, e, n), F32),
        compiler_params=_cparams("arbitrary"),
        name="route",
    )(aff_t, tri)


def _slot_index_kernel(pos_ref, idx_ref, *, cap, sc):
    pos = pos_ref[0, 0]
    n = pos.shape[1]
    for c in range(cap // sc):
        slot = (lax.broadcasted_iota(jnp.int32, (sc, LANES), 0) + c * sc).astype(F32)
        tok_acc = jnp.zeros((sc, LANES), F32)
        for j in range(n // LANES):
            tok = (lax.broadcasted_iota(jnp.int32, (1, LANES), 1) + j * LANES).astype(F32)
            tok_acc = tok_acc + jnp.where(slot == pos[:, j * LANES:(j + 1) * LANES], tok, 0.0)
        idx_ref[0, 0, :, c * sc:(c + 1) * sc] = jnp.sum(tok_acc.T, axis=0, keepdims=True).astype(jnp.int32)


def _slot_index(pos, cap):
    b, e, n = pos.shape
    sc = min(LANES, cap)
    sel = lambda bi, ei: (bi, ei, 0, 0)
    return pl.pallas_call(
        functools.partial(_slot_index_kernel, cap=cap, sc=sc),
        grid=(b, e),
        in_specs=[pl.BlockSpec((1, 1, 1, n), sel)],
        out_specs=pl.BlockSpec((1, 1, 1, cap), sel),
        out_shape=jax.ShapeDtypeStruct((b, e, 1, cap), jnp.int32),
        compiler_params=_cparams("arbitrary", "arbitrary"),
        name="slot_index",
    )(pos.reshape(b, e, 1, n))


def _gather_kernel(idx_ref, h3_ref, xe_ref, tile_ref, *, cap, nk, stride):
    for s in range(cap):
        t = idx_ref[0, 0, 0, s]
        tile_ref[pl.ds(s, nk, stride=stride), :] = h3_ref[0, pl.ds(pl.multiple_of(t * nk, nk), nk), :]
    for k in range(nk):
        xe_ref[0, 0, :, k * LANES:(k + 1) * LANES] = tile_ref[k * stride:k * stride + cap, :].astype(BF16)


def _gather(idx, h3, cap):
    b, e = idx.shape[:2]
    rows = h3.shape[1]
    nk = 8
    d = nk * LANES
    stride = cap + 8
    return pl.pallas_call(
        functools.partial(_gather_kernel, cap=cap, nk=nk, stride=stride),
        grid=(b, e),
        in_specs=[pl.BlockSpec((1, 1, 1, cap), lambda bi, ei: (bi, ei, 0, 0), memory_space=pltpu.SMEM),
                  pl.BlockSpec((1, rows, LANES), lambda bi, ei: (bi, 0, 0))],
        out_specs=pl.BlockSpec((1, 1, cap, d), lambda bi, ei: (ei, bi, 0, 0)),
        out_shape=jax.ShapeDtypeStruct((e, b, cap, d), BF16),
        scratch_shapes=[pltpu.VMEM((nk * stride, LANES), F32)],
        compiler_params=_cparams("arbitrary", "arbitrary"),
        name="moe_gather",
    )(idx, h3)


def _ffn_kernel(xe_ref, wg_ref, wu_ref, wd_ref, ye_ref, acc_ref, wg_bf, wu_bf, wd_bf, *, nf, tm):
    f = pl.program_id(1)
    m = pl.program_id(2)

    @pl.when(m == 0)
    def _():
        wg_bf[...] = wg_ref[0, 0].astype(BF16)
        wu_bf[...] = wu_ref[0, 0].astype(BF16)
        wd_bf[...] = wd_ref[0, 0].astype(BF16)

    rows = pl.ds(pl.multiple_of(m * tm, tm), tm)

    @pl.when(f == 0)
    def _():
        acc_ref[rows, :] = jnp.zeros((tm, acc_ref.shape[1]), F32)

    x = xe_ref[0]
    g = _dot(x, wg_bf[...])
    u = _dot(x, wu_bf[...])
    acc_ref[rows, :] += _dot((g * _sigmoid(g) * u).astype(BF16), wd_bf[...])

    @pl.when(f == nf - 1)
    def _():
        _store_token_major(ye_ref, acc_ref[rows, :])


def _expert_ffn(xe, w_gate, w_up, w_down, layer):
    e, m, d = xe.shape
    dff = w_gate.shape[-1]
    nk = d // LANES
    tm = min(1024, m)
    tf = 512
    nf = dff // tf
    out_tile = lambda ei, f, mi: (ei, jnp.where(f == nf - 1, mi, 0), 0)
    return pl.pallas_call(
        functools.partial(_ffn_kernel, nf=nf, tm=tm),
        grid=(e, nf, m // tm),
        in_specs=[pl.BlockSpec((1, tm, d), lambda ei, f, mi: (ei, mi, 0)),
                  pl.BlockSpec((1, 1, d, tf), lambda ei, f, mi: (layer, ei, 0, f)),
                  pl.BlockSpec((1, 1, d, tf), lambda ei, f, mi: (layer, ei, 0, f)),
                  pl.BlockSpec((1, 1, tf, d), lambda ei, f, mi: (layer, ei, f, 0))],
        out_specs=pl.BlockSpec((1, tm * nk, LANES), out_tile),
        out_shape=jax.ShapeDtypeStruct((e, m * nk, LANES), F32),
        scratch_shapes=[pltpu.VMEM((m, d), F32), pltpu.VMEM((d, tf), BF16), pltpu.VMEM((d, tf), BF16),
                        pltpu.VMEM((tf, d), BF16)],
        compiler_params=_cparams("arbitrary", "arbitrary", "arbitrary"),
        name="expert_ffn",
    )(xe, w_gate, w_up, w_down)


def _combine_kernel(idx_ref, aff_ref, ye_ref, x_ref, g2_ref, o_ref, acc_ref, *, cap, nk, tm, batch):
    k = pl.program_id(1)

    @pl.when(k == 0)
    def _():
        acc_ref[...] = jnp.zeros_like(acc_ref)

    @pl.when(k < N_EXPERTS)
    def _():
        for s0 in range(0, cap, batch):
            rows = []
            for s in range(s0, s0 + batch):
                t = idx_ref[0, 0, 0, s]
                start = pl.multiple_of(t * nk, nk)
                rows.append((start, acc_ref[pl.ds(start, nk), :] + aff_ref[0, 0, 0, t] * ye_ref[0, 0, s * nk:(s + 1) * nk, :]))
            for start, val in rows:
                acc_ref[pl.ds(start, nk), :] = val

    @pl.when(k >= N_EXPERTS)
    def _():
        view = acc_ref.at[pl.ds(pl.multiple_of((k - N_EXPERTS) * (tm * nk), tm * nk), tm * nk), :]
        for c in range(nk):
            lanes = slice(c * LANES, (c + 1) * LANES)
            o_ref[0, :, lanes] = x_ref[0, :, lanes] + g2_ref[0, :, lanes] * view[pl.ds(c, tm, stride=nk), :]


def _combine(idx, aff_t, ye3, x, g2, cap):
    b, n, d = x.shape
    e = idx.shape[1]
    nk = d // LANES
    tm = min(512, n)
    nt = n // tm
    smem = lambda width: pl.BlockSpec((1, 1, 1, width), lambda bi, k: (bi, jnp.minimum(k, e - 1), 0, 0),
                                      memory_space=pltpu.SMEM)
    tile = lambda bi, k: (bi, jnp.maximum(k - e, 0), 0)
    return pl.pallas_call(
        functools.partial(_combine_kernel, cap=cap, nk=nk, tm=tm, batch=8),
        grid=(b, e + nt),
        in_specs=[smem(cap), smem(n),
                  pl.BlockSpec((1, 1, cap * nk, LANES), lambda bi, k: (jnp.minimum(k, e - 1), bi, 0, 0)),
                  pl.BlockSpec((1, tm, d), tile),
                  pl.BlockSpec((1, 1, d), lambda bi, k: (bi, 0, 0))],
        out_specs=pl.BlockSpec((1, tm, d), tile),
        out_shape=jax.ShapeDtypeStruct((b, n, d), F32),
        scratch_shapes=[pltpu.VMEM((n * nk, LANES), F32)],
        compiler_params=_cparams("arbitrary", "arbitrary"),
        name="moe_combine",
    )(idx, aff_t.reshape(b, e, 1, n), ye3, x, g2)


def _moe(x, h3, aff_t, g2, w_gate, w_up, w_down, layer):
    b, n, d = x.shape
    nk = d // LANES
    cap = CAPACITY_FACTOR * n // N_EXPERTS
    pos = _route(aff_t, cap)
    idx = _slot_index(pos, cap)
    xe = _gather(idx, h3, cap)
    ye3 = _expert_ffn(xe.reshape(N_EXPERTS, b * cap, d), w_gate, w_up, w_down, layer)
    return _combine(idx, aff_t, ye3.reshape(N_EXPERTS, b, cap * nk, LANES), x, g2, cap)


def _rope_tables(n_tok):
    rows = n_tok // GRID_W
    row = np.repeat(np.arange(rows), GRID_W).astype(np.float32)
    col = np.tile(np.arange(GRID_W), rows).astype(np.float32)
    n_freq = HEAD_DIM // 4
    inv = jnp.asarray(ROPE_THETA, F32) ** (-jnp.arange(n_freq, dtype=F32) / n_freq)
    ang_r = jnp.asarray(row)[:, None] * inv
    ang_c = jnp.asarray(col)[:, None] * inv
    cr, sr, cc, sc = jnp.cos(ang_r), jnp.sin(ang_r), jnp.cos(ang_c), jnp.sin(ang_c)
    zero = jnp.zeros_like(sr)
    cos = jnp.concatenate([cr, cr, cc, cc], axis=-1)
    sin_a = jnp.concatenate([-sr, zero, -sc, zero], axis=-1)
    sin_b = jnp.concatenate([zero, sr, zero, sc], axis=-1)
    return tuple(jnp.tile(t, (1, N_Q_HEADS)) for t in (cos, sin_a, sin_b))


def _head_mean_matrix():
    blk = np.kron(np.eye(N_Q_HEADS, dtype=np.float32), np.full((HEAD_DIM, HEAD_DIM), 1.0 / HEAD_DIM, np.float32))
    return jnp.asarray(blk, BF16)


def _rows(m, b):
    return m[:b, None, :]


def kernel(x, c, ctx, c_ctx, norm_mix_g, norm_ffn_g, w_mod, b_mod, ev_w_in, ev_conv_w, ev_conv_b, ev_ln_g, ev_ln_b, ev_q_norm_g, ev_k_norm_g, ev_w_out, od_w_in, od_conv_w, od_pool_w, od_pool_scale, od_w_out, w_router, w_gate, w_up, w_down):
    b, l, d = x.shape
    depth = w_mod.shape[0]
    last_even = ((depth - 1) // 2) * 2

    n_rows = -(-(b + 1) // 8) * 8
    c_rows = jnp.zeros((n_rows, d), F32).at[:b].set(c).at[b].set(c_ctx)
    mods = _modulations(c_rows, w_mod, b_mod)
    rope = _rope_tables(l)
    bd = _head_mean_matrix()

    for i in range(depth):
        j = i // 2
        is_even = i % 2 == 0
        ctx_live = i < last_even
        m6 = mods[i].reshape(n_rows, 6, d)
        sh1, sc1, g1, sh2, sc2, g2 = [_rows(m6[:, t], b) for t in range(6)]
        mc = [jnp.broadcast_to(m6[b, t][None, None, :], (b, 1, d)) for t in range(6)]
        sh1c, sc1c, g1c, sh2c, sc2c, g2c = mc
        nm_g = norm_mix_g[i][None, :]
        nf_g = norm_ffn_g[i][None, :]
        wr_hi, wr_lo = _split_bf16(jnp.pad(w_router[i], ((0, 0), (0, LANES - N_EXPERTS))))
        wr_lo = jnp.concatenate([wr_hi, wr_lo], axis=1)

        if is_even:
            w_in = ev_w_in[j].astype(BF16)
            wo = ev_w_out[j].astype(BF16)
            qg = jnp.tile(ev_q_norm_g[j], N_Q_HEADS)[None, :]
            kg = jnp.tile(ev_k_norm_g[j], N_KV_HEADS)[None, :]
            conv_args = (ev_conv_w[j], ev_conv_b[j][None, :], ev_ln_g[j][None, :], ev_ln_b[j][None, :])
            u, q, k, v = _even_in(x, sh1, sc1, nm_g, w_in, qg, kg, bd, rope)
            uc, qc, kc, vc = _even_in(ctx, sh1c, sc1c, nm_g, w_in, qg, kg, bd, None)
            attn = _attention(q, jnp.concatenate([kc, k], axis=1), jnp.concatenate([vc, v], axis=1))
            conv = _conf_conv(u, *conv_args)
            x, h2, aff = _even_out(conv, attn, wo, x, g1, nf_g, sh2, sc2, wr_hi, wr_lo)
            if ctx_live:
                attn_c = _attention(qc, kc, vc)
                conv_c = _conf_conv(uc, *conv_args)
                ctx, h2c, affc = _even_out(conv_c, attn_c, wo, ctx, g1c, nf_g, sh2c, sc2c, wr_hi, wr_lo)
        else:
            w_in = od_w_in[j].astype(BF16)
            wo = od_w_out[j].astype(BF16)
            mix_args = (od_conv_w[j], od_pool_w[j].astype(BF16), od_pool_scale[j][None, :], wo)
            z = _odd_in(x, sh1, sc1, nm_g, w_in)
            x, h2, aff = _odd_mix(z, *mix_args, x, g1, nf_g, sh2, sc2, wr_hi, wr_lo)
            if ctx_live:
                zc = _odd_in(ctx, sh1c, sc1c, nm_g, w_in)
                ctx, h2c, affc = _odd_mix(zc, *mix_args, ctx, g1c, nf_g, sh2c, sc2c, wr_hi, wr_lo)
        x = _moe(x, h2, aff, g2, w_gate, w_up, w_down, i)
        if ctx_live:
            ctx = _moe(ctx, h2c, affc, g2c, w_gate, w_up, w_down, i)
    return x
```

```python
import functools
import math

import jax
import jax.numpy as jnp
import numpy as np
from jax import lax
from jax.experimental import pallas as pl
from jax.experimental.pallas import tpu as pltpu

F32 = jnp.float32
BF16 = jnp.bfloat16

GRID_W = 64
N_Q_HEADS = 8
N_KV_HEADS = 2
HEAD_DIM = 64
D_ATTN = N_Q_HEADS * HEAD_DIM
D_KV = N_KV_HEADS * HEAD_DIM
ROPE_THETA = 10000.0
D_CONF = 512
D_SHORT = 512
D_POOL = 512
POOL_WINDOWS = (2, 4, 8, 16)
POOL_GROUP = D_POOL // len(POOL_WINDOWS)
N_EXPERTS = 16
CAPACITY_FACTOR = 2
EPS = 1e-6

LANES = 128
HALO = 16
VMEM_LIMIT = 56 * 1024 * 1024


def _sigmoid(x):
    return 1.0 / (1.0 + jnp.exp(-x))


def _dot(a, b):
    return jnp.dot(a, b, preferred_element_type=F32)


def _split_bf16(x):
    hi = x.astype(BF16)
    lo = (x - hi.astype(F32)).astype(BF16)
    return hi, lo


def _cparams(*sem):
    return pltpu.CompilerParams(dimension_semantics=sem, vmem_limit_bytes=VMEM_LIMIT)


def _mod_kernel(c_ref, w_ref, b_ref, o_ref):
    c = c_ref[...]
    s_hi, s_lo = _split_bf16(c * _sigmoid(c))
    w_hi, w_lo = _split_bf16(w_ref[0])
    o_ref[0] = _dot(s_hi, w_hi) + _dot(s_hi, w_lo) + _dot(s_lo, w_hi) + b_ref[0]


def _modulations(c_rows, w_mod, b_mod):
    depth, d, n = w_mod.shape
    r = c_rows.shape[0]
    tn = 1536
    return pl.pallas_call(
        _mod_kernel,
        grid=(depth, n // tn),
        in_specs=[pl.BlockSpec((r, d), lambda i, j: (0, 0)),
                  pl.BlockSpec((1, d, tn), lambda i, j: (i, 0, j)),
                  pl.BlockSpec((1, 1, tn), lambda i, j: (i, 0, j))],
        out_specs=pl.BlockSpec((1, r, tn), lambda i, j: (i, 0, j)),
        out_shape=jax.ShapeDtypeStruct((depth, r, n), F32),
        compiler_params=_cparams("arbitrary", "arbitrary"),
        name="modulations",
    )(c_rows, w_mod, b_mod.reshape(depth, 1, n))


def _modulated_norm(x, g, sh, sc):
    ms = jnp.mean(x * x, axis=-1, keepdims=True)
    return x * lax.rsqrt(ms + EPS) * g * (1.0 + sc) + sh


def _store_token_major(ref3, val):
    rows, d = val.shape
    nk = d // LANES
    for k in range(nk):
        ref3[0, pl.ds(k, rows, stride=nk), :] = val[:, k * LANES:(k + 1) * LANES]


def _post(x, y, g1, nf_g, sh2, sc2, wr_hi, wr_hilo, xo_ref, h3_ref, aff_ref):
    xn = x + g1 * y
    xo_ref[0] = xn
    h = _modulated_norm(xn, nf_g, sh2, sc2)
    h_hi, h_lo = _split_bf16(h)
    _store_token_major(h3_ref, h)
    both = _dot(h_hi, wr_hilo)
    lg = both[:, :LANES] + both[:, LANES:] + _dot(h_lo, wr_hi)
    lgt = lg.T[:N_EXPERTS]
    ex = jnp.exp(lgt - jnp.max(lgt, axis=0, keepdims=True))
    aff_ref[0] = ex / jnp.sum(ex, axis=0, keepdims=True)


def _head_rms(t, gain, bd):
    hi, lo = _split_bf16(t * t)
    ms = _dot(hi, bd) + _dot(lo, bd)
    return t * lax.rsqrt(ms + EPS) * gain


def _rope(t, cos, sin_a, sin_b):
    w = t.shape[-1]
    q = HEAD_DIM // 4
    return t * cos + pltpu.roll(t, w - q, 1) * sin_a + pltpu.roll(t, q, 1) * sin_b


def _even_in_kernel(*refs, rope):
    if rope:
        (x_ref, sh_ref, sc_ref, g_ref, w_ref, qg_ref, kg_ref, bd_ref, cos_ref, sa_ref, sb_ref,
         u_ref, q_ref, k_ref, v_ref) = refs
    else:
        x_ref, sh_ref, sc_ref, g_ref, w_ref, qg_ref, kg_ref, bd_ref, u_ref, q_ref, k_ref, v_ref = refs
    h = _modulated_norm(x_ref[0], g_ref[...], sh_ref[0], sc_ref[0])
    p = _dot(h.astype(BF16), w_ref[...])
    kv0 = 2 * D_CONF + D_ATTN
    u_ref[0] = p[:, :D_CONF] * _sigmoid(p[:, D_CONF:2 * D_CONF])
    q = _head_rms(p[:, 2 * D_CONF:kv0], qg_ref[...], bd_ref[...])
    k = _head_rms(p[:, kv0:kv0 + D_KV], kg_ref[...], bd_ref[:D_KV, :D_KV])
    if rope:
        cos, sa, sb = cos_ref[...], sa_ref[...], sb_ref[...]
        q = _rope(q, cos, sa, sb)
        k = _rope(k, cos[:, :D_KV], sa[:, :D_KV], sb[:, :D_KV])
    q_ref[0] = (q * (HEAD_DIM ** -0.5)).astype(BF16)
    k_ref[0] = k.astype(BF16)
    v_ref[0] = p[:, kv0 + D_KV:].astype(BF16)


def _even_in(x, sh, sc, g, w_bf, qg, kg, bd, rope_tabs):
    b, l, d = x.shape
    tm = min(512, l)
    n_in = w_bf.shape[1]
    row = lambda bi, t: (bi, 0, 0)
    tok = lambda bi, t: (bi, t, 0)
    const = lambda bi, t: (0, 0)
    in_specs = [pl.BlockSpec((1, tm, d), tok), pl.BlockSpec((1, 1, d), row), pl.BlockSpec((1, 1, d), row),
                pl.BlockSpec((1, d), const), pl.BlockSpec((d, n_in), const),
                pl.BlockSpec((1, D_ATTN), const), pl.BlockSpec((1, D_KV), const),
                pl.BlockSpec((D_ATTN, D_ATTN), const)]
    args = [x, sh, sc, g, w_bf, qg, kg, bd]
    if rope_tabs is not None:
        in_specs += [pl.BlockSpec((tm, D_ATTN), lambda bi, t: (t, 0))] * 3
        args += list(rope_tabs)
    return pl.pallas_call(
        functools.partial(_even_in_kernel, rope=rope_tabs is not None),
        grid=(b, l // tm),
        in_specs=in_specs,
        out_specs=[pl.BlockSpec((1, tm, D_CONF), tok), pl.BlockSpec((1, tm, D_ATTN), tok),
                   pl.BlockSpec((1, tm, D_KV), tok), pl.BlockSpec((1, tm, D_KV), tok)],
        out_shape=[jax.ShapeDtypeStruct((b, l, D_CONF), F32), jax.ShapeDtypeStruct((b, l, D_ATTN), BF16),
                   jax.ShapeDtypeStruct((b, l, D_KV), BF16), jax.ShapeDtypeStruct((b, l, D_KV), BF16)],
        compiler_params=_cparams("arbitrary", "arbitrary"),
        name="even_in_proj",
    )(*args)


def _fill_halo(xs_ref, prev_ref, cur_ref, next_ref, tm, nt):
    t = pl.program_id(1)
    xs_ref[0:HALO] = jnp.where(t > 0, prev_ref[0], 0.0)
    xs_ref[HALO:HALO + tm] = cur_ref[0]
    xs_ref[HALO + tm:HALO + tm + HALO] = jnp.where(t < nt - 1, next_ref[0], 0.0)


def _conf_conv_kernel(up_ref, uc_ref, un_ref, w_ref, b_ref, lg_ref, lb_ref, o_ref, xs_ref, sh_ref, *, tm, nt, rc):
    _fill_halo(xs_ref, up_ref, uc_ref, un_ref, tm, nt)
    width = w_ref.shape[0]
    pad = width // 2
    span = sh_ref.shape[1]
    for s in range(1, 8):
        sh_ref[s - 1] = xs_ref[pl.ds(s, span), :]
    for r in range(tm // rc):
        acc = jnp.zeros((rc, D_CONF), F32)
        for k in range(width):
            a, s = divmod(HALO - pad + k, 8)
            rows = pl.ds(r * rc + 8 * a, rc)
            win = xs_ref[rows, :] if s == 0 else sh_ref[s - 1, rows, :]
            acc = acc + win * w_ref[k:k + 1, :]
        u = acc + b_ref[...]
        mu = jnp.mean(u, axis=-1, keepdims=True)
        ctr = u - mu
        var = jnp.mean(ctr * ctr, axis=-1, keepdims=True)
        un = ctr * lax.rsqrt(var + EPS) * lg_ref[...] + lb_ref[...]
        o_ref[0, r * rc:(r + 1) * rc, :] = (un * _sigmoid(un)).astype(BF16)


def _halo_specs(tm, l, c):
    hb = tm // HALO
    last = l // HALO - 1
    return [pl.BlockSpec((1, HALO, c), lambda bi, t: (bi, jnp.maximum(t * hb - 1, 0), 0)),
            pl.BlockSpec((1, tm, c), lambda bi, t: (bi, t, 0)),
            pl.BlockSpec((1, HALO, c), lambda bi, t: (bi, jnp.minimum((t + 1) * hb, last), 0))]


def _conf_conv(u, conv_w, conv_b, ln_g, ln_b):
    b, l, c = u.shape
    tm = min(256, l)
    nt = l // tm
    const = lambda bi, t: (0, 0)
    return pl.pallas_call(
        functools.partial(_conf_conv_kernel, tm=tm, nt=nt, rc=32),
        grid=(b, nt),
        in_specs=_halo_specs(tm, l, c) + [pl.BlockSpec(conv_w.shape, const)] + [pl.BlockSpec((1, c), const)] * 3,
        out_specs=pl.BlockSpec((1, tm, c), lambda bi, t: (bi, t, 0)),
        out_shape=jax.ShapeDtypeStruct((b, l, c), BF16),
        scratch_shapes=[pltpu.VMEM((tm + 2 * HALO, c), F32), pltpu.VMEM((7, tm + 2 * HALO - 8, c), F32)],
        compiler_params=_cparams("arbitrary", "arbitrary"),
        name="conformer_conv",
    )(u, u, u, conv_w, conv_b, ln_g, ln_b)


def _attn_kernel(q_ref, kt_ref, v_ref, o_ref, *, chunks):
    q = q_ref[0]
    tq = q.shape[0]
    halves = []
    for half in range(2):
        m = jnp.full((tq, 1), -jnp.inf, F32)
        acc = jnp.zeros((tq, LANES), F32)
        for c0, cs in chunks:
            s = _dot(q, kt_ref[0, 0, half, :, c0:c0 + cs])
            m_new = jnp.maximum(m, jnp.max(s, axis=1, keepdims=True))
            p = jnp.exp(s - m_new)
            acc = jnp.exp(m - m_new) * acc + _dot(p.astype(BF16), v_ref[0, 0, half, c0:c0 + cs, :])
            m = m_new
        den = (1 - half) * HEAD_DIM
        halves.append(acc / acc[:, den:den + 1])
    lane = lax.broadcasted_iota(jnp.int32, (tq, LANES), 1)
    o_ref[0] = jnp.where(lane < HEAD_DIM, halves[0], halves[1]).astype(BF16)


def _kv_layouts(k, v):
    b, lk, _ = k.shape
    kt = k.reshape(b, lk, N_KV_HEADS, HEAD_DIM).transpose(0, 2, 3, 1)
    z = jnp.zeros_like(kt)
    kt = jnp.stack([jnp.concatenate([kt, z], axis=2), jnp.concatenate([z, kt], axis=2)], axis=2)
    vh = v.reshape(b, lk, N_KV_HEADS, HEAD_DIM).transpose(0, 2, 1, 3)
    pad = jnp.zeros_like(vh).at[..., 0].set(1.0)
    vv = jnp.stack([jnp.concatenate([vh, pad], axis=3), jnp.concatenate([pad, vh], axis=3)], axis=2)
    return kt, vv


def _key_chunks(lk, size=1024):
    head = lk % size
    chunks = [(0, head)] if head else []
    return tuple(chunks + [(c, size) for c in range(head, lk, size)])


def _attention(q, k, v):
    b, l, _ = q.shape
    lk = k.shape[1]
    kt, vv = _kv_layouts(k, v)
    tq = min(1024, l)
    n_pairs = D_ATTN // LANES
    pairs_per_kv = n_pairs // N_KV_HEADS
    return pl.pallas_call(
        functools.partial(_attn_kernel, chunks=_key_chunks(lk)),
        grid=(b, n_pairs, l // tq),
        in_specs=[pl.BlockSpec((1, tq, LANES), lambda bi, j, t: (bi, t, j)),
                  pl.BlockSpec((1, 1, 2, LANES, lk), lambda bi, j, t: (bi, j // pairs_per_kv, 0, 0, 0)),
                  pl.BlockSpec((1, 1, 2, lk, LANES), lambda bi, j, t: (bi, j // pairs_per_kv, 0, 0, 0))],
        out_specs=pl.BlockSpec((1, tq, LANES), lambda bi, j, t: (bi, t, j)),
        out_shape=jax.ShapeDtypeStruct((b, l, D_ATTN), BF16),
        compiler_params=_cparams("arbitrary", "arbitrary", "arbitrary"),
        name="attention",
    )(q, kt, vv)


def _even_out_kernel(cv_ref, at_ref, wo_ref, x_ref, g1_ref, nfg_ref, sh2_ref, sc2_ref, wrh_ref, wrl_ref,
                     xo_ref, h2_ref, aff_ref):
    y = _dot(cv_ref[0], wo_ref[:D_CONF, :]) + _dot(at_ref[0], wo_ref[D_CONF:, :])
    _post(x_ref[0], y, g1_ref[0], nfg_ref[...], sh2_ref[0], sc2_ref[0], wrh_ref[...], wrl_ref[...],
          xo_ref, h2_ref, aff_ref)


def _post_specs(tm, d):
    row = lambda bi, t: (bi, 0, 0)
    const = lambda bi, t: (0, 0)
    in_specs = [pl.BlockSpec((1, tm, d), lambda bi, t: (bi, t, 0)), pl.BlockSpec((1, 1, d), row),
                pl.BlockSpec((1, d), const), pl.BlockSpec((1, 1, d), row), pl.BlockSpec((1, 1, d), row),
                pl.BlockSpec((d, LANES), const), pl.BlockSpec((d, 2 * LANES), const)]
    nk = d // LANES
    out_specs = [pl.BlockSpec((1, tm, d), lambda bi, t: (bi, t, 0)),
                 pl.BlockSpec((1, tm * nk, LANES), lambda bi, t: (bi, t, 0)),
                 pl.BlockSpec((1, N_EXPERTS, tm), lambda bi, t: (bi, 0, t))]
    return in_specs, out_specs


def _post_out_shapes(b, l, d):
    return [jax.ShapeDtypeStruct((b, l, d), F32), jax.ShapeDtypeStruct((b, l * (d // LANES), LANES), F32),
            jax.ShapeDtypeStruct((b, N_EXPERTS, l), F32)]


def _even_out(conv, attn, wo_bf, x, g1, nf_g, sh2, sc2, wr_hi, wr_lo):
    b, l, d = x.shape
    tm = min(512, l)
    tok = lambda bi, t: (bi, t, 0)
    post_in, post_out = _post_specs(tm, d)
    return pl.pallas_call(
        _even_out_kernel,
        grid=(b, l // tm),
        in_specs=[pl.BlockSpec((1, tm, D_CONF), tok), pl.BlockSpec((1, tm, D_ATTN), tok),
                  pl.BlockSpec(wo_bf.shape, lambda bi, t: (0, 0))] + post_in,
        out_specs=post_out,
        out_shape=_post_out_shapes(b, l, d),
        compiler_params=_cparams("arbitrary", "arbitrary"),
        name="even_out_proj",
    )(conv, attn, wo_bf, x, g1, nf_g, sh2, sc2, wr_hi, wr_lo)


def _odd_in_kernel(x_ref, sh_ref, sc_ref, g_ref, w_ref, z_ref):
    h = _modulated_norm(x_ref[0], g_ref[...], sh_ref[0], sc_ref[0])
    p = _dot(h.astype(BF16), w_ref[...])
    ds_ = D_SHORT
    z_ref[0, :, :ds_] = p[:, 2 * ds_:3 * ds_] * p[:, :ds_]
    z_ref[0, :, ds_:2 * ds_] = p[:, 3 * ds_:]
    z_ref[0, :, 2 * ds_:] = p[:, ds_:2 * ds_]


def _odd_in(x, sh, sc, g, w_bf):
    b, l, d = x.shape
    tm = min(512, l)
    row = lambda bi, t: (bi, 0, 0)
    tok = lambda bi, t: (bi, t, 0)
    const = lambda bi, t: (0, 0)
    nz = 2 * D_SHORT + D_POOL
    return pl.pallas_call(
        _odd_in_kernel,
        grid=(b, l // tm),
        in_specs=[pl.BlockSpec((1, tm, d), tok), pl.BlockSpec((1, 1, d), row), pl.BlockSpec((1, 1, d), row),
                  pl.BlockSpec((1, d), const), pl.BlockSpec(w_bf.shape, const)],
        out_specs=pl.BlockSpec((1, tm, nz), tok),
        out_shape=jax.ShapeDtypeStruct((b, l, nz), F32),
        compiler_params=_cparams("arbitrary", "arbitrary"),
        name="odd_in_proj",
    )(x, sh, sc, g, w_bf)


def _odd_mix_kernel(zp_ref, zc_ref, zn_ref, cw_ref, pw_ref, ps_ref, wo_ref,
                    x_ref, g1_ref, nfg_ref, sh2_ref, sc2_ref, wrh_ref, wrl_ref,
                    xo_ref, h2_ref, aff_ref, xs_ref, *, tm, nt, n_tok):
    _fill_halo(xs_ref, zp_ref, zc_ref, zn_ref, tm, nt)
    ds_ = D_SHORT
    width = cw_ref.shape[0]
    pad = width // 2
    conv = jnp.zeros((tm, ds_), F32)
    for k in range(width):
        conv = conv + xs_ref[pl.ds(HALO - pad + k, tm), 0:ds_] * cw_ref[k:k + 1, :]
    short = xs_ref[pl.ds(HALO, tm), 2 * ds_:3 * ds_] * conv
    pos = pl.program_id(1) * tm + lax.broadcasted_iota(jnp.int32, (tm, POOL_GROUP), 0)
    pooled = []
    for gi, w in enumerate(POOL_WINDOWS):
        c0 = ds_ + gi * POOL_GROUP
        tot = jnp.zeros((tm, POOL_GROUP), F32)
        for dlt in range(-(w // 2), w // 2):
            tot = tot + xs_ref[pl.ds(HALO + dlt, tm), c0:c0 + POOL_GROUP]
        cnt = (jnp.minimum(pos + w // 2, n_tok) - jnp.maximum(pos - w // 2, 0)).astype(F32)
        diff = tot / cnt - xs_ref[pl.ds(HALO, tm), c0:c0 + POOL_GROUP]
        pooled.append(_dot(diff.astype(BF16), pw_ref[gi]))
    pool = jnp.concatenate(pooled, axis=-1) * ps_ref[...]
    y = _dot(short.astype(BF16), wo_ref[:ds_, :]) + _dot(pool.astype(BF16), wo_ref[ds_:, :])
    _post(x_ref[0], y, g1_ref[0], nfg_ref[...], sh2_ref[0], sc2_ref[0], wrh_ref[...], wrl_ref[...],
          xo_ref, h2_ref, aff_ref)


def _odd_mix(z, conv_w, pool_w_bf, pool_scale, wo_bf, x, g1, nf_g, sh2, sc2, wr_hi, wr_lo):
    b, l, d = x.shape
    tm = min(256, l)
    nt = l // tm
    nz = z.shape[-1]
    const = lambda bi, t: (0, 0)
    post_in, post_out = _post_specs(tm, d)
    return pl.pallas_call(
        functools.partial(_odd_mix_kernel, tm=tm, nt=nt, n_tok=l),
        grid=(b, nt),
        in_specs=_halo_specs(tm, l, nz) + [pl.BlockSpec(conv_w.shape, const),
                                            pl.BlockSpec(pool_w_bf.shape, lambda bi, t: (0, 0, 0)),
                                            pl.BlockSpec((1, D_POOL), const),
                                            pl.BlockSpec(wo_bf.shape, const)] + post_in,
        out_specs=post_out,
        out_shape=_post_out_shapes(b, l, d),
        scratch_shapes=[pltpu.VMEM((tm + 2 * HALO, nz), F32)],
        compiler_params=_cparams("arbitrary", "arbitrary"),
        name="odd_mixer",
    )(z, z, z, conv_w, pool_w_bf, pool_scale, wo_bf, x, g1, nf_g, sh2, sc2, wr_hi, wr_lo)


def _route_kernel(aff_ref, tri_ref, pos_ref, *, cap):
    key = lax.bitcast_convert_type(aff_ref[0], jnp.int32)
    n = key.shape[1]
    capf = float(cap)
    thr = jnp.zeros((N_EXPERTS, 1), jnp.int32)
    for bit in range(30, -1, -1):
        cand = thr | (1 << bit)
        cnt = jnp.sum(jnp.where(key >= cand, 1.0, 0.0), axis=1, keepdims=True)
        thr = jnp.where(cnt >= capf, cand, thr)
    n_gt = jnp.sum(jnp.where(key > thr, 1.0, 0.0), axis=1, keepdims=True)
    need = capf - n_gt
    tri = tri_ref[...]
    off_eq = jnp.zeros((N_EXPERTS, 1), F32)
    off_sel = jnp.zeros((N_EXPERTS, 1), F32)
    for j in range(n // LANES):
        kj = key[:, j * LANES:(j + 1) * LANES]
        gt = kj > thr
        eqf = jnp.where(kj == thr, 1.0, 0.0)
        rank = _dot(eqf.astype(BF16), tri) + off_eq - eqf
        self_ = jnp.where(gt, 1.0, jnp.where(rank < need, eqf, 0.0))
        slot = _dot(self_.astype(BF16), tri) + off_sel - 1.0
        pos_ref[0, :, j * LANES:(j + 1) * LANES] = jnp.where(self_ > 0.0, slot, -1.0)
        off_eq = off_eq + jnp.sum(eqf, axis=1, keepdims=True)
        off_sel = off_sel + jnp.sum(self_, axis=1, keepdims=True)


def _route(aff_t, cap):
    b, e, n = aff_t.shape
    tri = jnp.asarray(np.triu(np.ones((LANES, LANES), np.float32)), BF16)
    return pl.pallas_call(
        functools.partial(_route_kernel, cap=cap),
        grid=(b,),
        in_specs=[pl.BlockSpec((1, e, n), lambda bi: (bi, 0, 0)), pl.BlockSpec((LANES, LANES), lambda bi: (0, 0))],
        out_specs=pl.BlockSpec((1, e, n), lambda bi: (bi, 0, 0)),
        out_shape=jax.ShapeDtypeStruct((b, e, n), F32),
        compiler_params=_cparams("arbitrary"),
        name="route",
    )(aff_t, tri)


ROWS_PER_STEP = 1024


def _experts_per_step(cap):
    return max(1, min(N_EXPERTS, ROWS_PER_STEP // cap))


def _slot_index_kernel(pos_ref, idx_ref, *, cap, sc, eps):
    n = pos_ref.shape[3]
    for ee in range(eps):
        pos = pos_ref[0, ee]
        for c in range(cap // sc):
            slot = (lax.broadcasted_iota(jnp.int32, (sc, LANES), 0) + c * sc).astype(F32)
            tok_acc = jnp.zeros((sc, LANES), F32)
            for j in range(n // LANES):
                tok = (lax.broadcasted_iota(jnp.int32, (1, LANES), 1) + j * LANES).astype(F32)
                tok_acc = tok_acc + jnp.where(slot == pos[:, j * LANES:(j + 1) * LANES], tok, 0.0)
            idx_ref[0, ee, :, c * sc:(c + 1) * sc] = jnp.sum(tok_acc.T, axis=0, keepdims=True).astype(jnp.int32)


def _slot_index(pos, cap):
    b, e, n = pos.shape
    sc = min(LANES, cap)
    eps = _experts_per_step(cap)
    sel = lambda bi, ei: (bi, ei, 0, 0)
    return pl.pallas_call(
        functools.partial(_slot_index_kernel, cap=cap, sc=sc, eps=eps),
        grid=(b, e // eps),
        in_specs=[pl.BlockSpec((1, eps, 1, n), sel)],
        out_specs=pl.BlockSpec((1, eps, 1, cap), sel),
        out_shape=jax.ShapeDtypeStruct((b, e, 1, cap), jnp.int32),
        compiler_params=_cparams("arbitrary", "arbitrary"),
        name="slot_index",
    )(pos.reshape(b, e, 1, n))


def _gather_kernel(idx_ref, h3_ref, *rest, cap, nk, stride, eps):
    xe_ref, tile_ref = rest[-2:]
    for ee in range(eps):
        for s in range(cap):
            t = idx_ref[0, ee, 0, s]
            tile_ref[ee, pl.ds(s, nk, stride=stride), :] = h3_ref[0, pl.ds(pl.multiple_of(t * nk, nk), nk), :]
        for k in range(nk):
            xe_ref[ee, :, k * LANES:(k + 1) * LANES] = tile_ref[ee, k * stride:k * stride + cap, :].astype(BF16)


def _gather(idx, h3, cap, total_rows, row0, xe_all=None):
    b, e = idx.shape[:2]
    rows = h3.shape[1]
    nk = 8
    d = nk * LANES
    stride = cap + 8
    eps = _experts_per_step(cap)
    blk0 = row0 // cap
    in_specs = [pl.BlockSpec((1, eps, 1, cap), lambda bi, ei: (bi, ei, 0, 0), memory_space=pltpu.SMEM),
                pl.BlockSpec((1, rows, LANES), lambda bi, ei: (bi, 0, 0))]
    args = [idx, h3]
    aliases = {}
    if xe_all is not None:
        in_specs.append(pl.BlockSpec(memory_space=pl.ANY))
        args.append(xe_all)
        aliases = {2: 0}
    return pl.pallas_call(
        functools.partial(_gather_kernel, cap=cap, nk=nk, stride=stride, eps=eps),
        grid=(b, e // eps),
        in_specs=in_specs,
        out_specs=pl.BlockSpec((eps, cap, d), lambda bi, ei: (ei, blk0 + bi, 0)),
        out_shape=jax.ShapeDtypeStruct((e, total_rows, d), BF16),
        scratch_shapes=[pltpu.VMEM((eps, nk * stride, LANES), F32)],
        input_output_aliases=aliases,
        compiler_params=_cparams("arbitrary", "arbitrary"),
        name="moe_gather",
    )(*args)


def _ffn_kernel(xe_ref, wg_ref, wu_ref, wd_ref, ye_ref, acc_ref, wg_bf, wu_bf, wd_bf, *, nf, tm):
    f = pl.program_id(1)
    m = pl.program_id(2)

    @pl.when(m == 0)
    def _():
        wg_bf[...] = wg_ref[0, 0].astype(BF16)
        wu_bf[...] = wu_ref[0, 0].astype(BF16)
        wd_bf[...] = wd_ref[0, 0].astype(BF16)

    rows = pl.ds(pl.multiple_of(m * tm, tm), tm)

    @pl.when(f == 0)
    def _():
        acc_ref[rows, :] = jnp.zeros((tm, acc_ref.shape[1]), F32)

    x = xe_ref[0]
    g = _dot(x, wg_bf[...])
    u = _dot(x, wu_bf[...])
    acc_ref[rows, :] += _dot((g * _sigmoid(g) * u).astype(BF16), wd_bf[...])

    @pl.when(f == nf - 1)
    def _():
        _store_token_major(ye_ref, acc_ref[rows, :])


def _expert_ffn(xe, w_gate, w_up, w_down, layer):
    e, m, d = xe.shape
    dff = w_gate.shape[-1]
    nk = d // LANES
    tm = max(t for t in range(16, min(m, 1152) + 1, 16) if m % t == 0)
    tf = 512
    nf = dff // tf
    out_tile = lambda ei, f, mi: (ei, jnp.where(f == nf - 1, mi, 0), 0)
    return pl.pallas_call(
        functools.partial(_ffn_kernel, nf=nf, tm=tm),
        grid=(e, nf, m // tm),
        in_specs=[pl.BlockSpec((1, tm, d), lambda ei, f, mi: (ei, mi, 0)),
                  pl.BlockSpec((1, 1, d, tf), lambda ei, f, mi: (layer, ei, 0, f)),
                  pl.BlockSpec((1, 1, d, tf), lambda ei, f, mi: (layer, ei, 0, f)),
                  pl.BlockSpec((1, 1, tf, d), lambda ei, f, mi: (layer, ei, f, 0))],
        out_specs=pl.BlockSpec((1, tm * nk, LANES), out_tile),
        out_shape=jax.ShapeDtypeStruct((e, m * nk, LANES), F32),
        scratch_shapes=[pltpu.VMEM((m, d), F32), pltpu.VMEM((d, tf), BF16), pltpu.VMEM((d, tf), BF16),
                        pltpu.VMEM((tf, d), BF16)],
        compiler_params=_cparams("arbitrary", "arbitrary", "arbitrary"),
        name="expert_ffn",
    )(xe, w_gate, w_up, w_down)


def _combine_kernel(idx_ref, aff_ref, ye_ref, x_ref, g2_ref, o_ref, acc_ref, *, cap, nk, tm, batch, eps, n_scatter):
    k = pl.program_id(1)

    @pl.when(k == 0)
    def _():
        acc_ref[...] = jnp.zeros_like(acc_ref)

    @pl.when(k < n_scatter)
    def _():
        for ee in range(eps):
            for s0 in range(0, cap, batch):
                rows = []
                for s in range(s0, s0 + batch):
                    t = idx_ref[0, ee, 0, s]
                    start = pl.multiple_of(t * nk, nk)
                    rows.append((start, acc_ref[pl.ds(start, nk), :]
                                 + aff_ref[0, ee, 0, t] * ye_ref[ee, s * nk:(s + 1) * nk, :]))
                for start, val in rows:
                    acc_ref[pl.ds(start, nk), :] = val

    @pl.when(k >= n_scatter)
    def _():
        view = acc_ref.at[pl.ds(pl.multiple_of((k - n_scatter) * (tm * nk), tm * nk), tm * nk), :]
        for c in range(nk):
            lanes = slice(c * LANES, (c + 1) * LANES)
            o_ref[0, :, lanes] = x_ref[0, :, lanes] + g2_ref[0, :, lanes] * view[pl.ds(c, tm, stride=nk), :]


def _combine(idx, aff_t, ye3, x, g2, cap, row0):
    b, n, d = x.shape
    e = idx.shape[1]
    nk = d // LANES
    tm = min(1024, n)
    nt = n // tm
    eps = _experts_per_step(cap)
    ns = e // eps
    blk0 = row0 // cap
    group = lambda bi, k: (bi, jnp.minimum(k, ns - 1), 0, 0)
    smem = lambda width: pl.BlockSpec((1, eps, 1, width), group, memory_space=pltpu.SMEM)
    tile = lambda bi, k: (bi, jnp.maximum(k - ns, 0), 0)
    return pl.pallas_call(
        functools.partial(_combine_kernel, cap=cap, nk=nk, tm=tm, batch=8, eps=eps, n_scatter=ns),
        grid=(b, ns + nt),
        in_specs=[smem(cap), smem(n),
                  pl.BlockSpec((eps, cap * nk, LANES), lambda bi, k: (jnp.minimum(k, ns - 1), blk0 + bi, 0)),
                  pl.BlockSpec((1, tm, d), tile),
                  pl.BlockSpec((1, 1, d), lambda bi, k: (bi, 0, 0))],
        out_specs=pl.BlockSpec((1, tm, d), tile),
        out_shape=jax.ShapeDtypeStruct((b, n, d), F32),
        scratch_shapes=[pltpu.VMEM((n * nk, LANES), F32)],
        compiler_params=_cparams("arbitrary", "arbitrary"),
        name="moe_combine",
    )(idx, aff_t.reshape(b, e, 1, n), ye3, x, g2)


def _moe(streams, w_gate, w_up, w_down, layer):
    caps = [CAPACITY_FACTOR * x.shape[1] // N_EXPERTS for x, _, _, _ in streams]
    row0s = list(np.cumsum([0] + [x.shape[0] * cap for (x, _, _, _), cap in zip(streams, caps)]))
    total_rows = int(row0s[-1])
    idxs, xe = [], None
    for (x, h3, aff_t, _), cap, row0 in zip(streams, caps, row0s):
        idx = _slot_index(_route(aff_t, cap), cap)
        xe = _gather(idx, h3, cap, total_rows, int(row0), xe)
        idxs.append(idx)
    ye3 = _expert_ffn(xe, w_gate, w_up, w_down, layer)
    return [_combine(idx, aff_t, ye3, x, g2, cap, int(row0))
            for (x, _, aff_t, g2), idx, cap, row0 in zip(streams, idxs, caps, row0s)]


def _rope_tables(n_tok):
    rows = n_tok // GRID_W
    row = np.repeat(np.arange(rows), GRID_W).astype(np.float32)
    col = np.tile(np.arange(GRID_W), rows).astype(np.float32)
    n_freq = HEAD_DIM // 4
    inv = jnp.asarray(ROPE_THETA, F32) ** (-jnp.arange(n_freq, dtype=F32) / n_freq)
    ang_r = jnp.asarray(row)[:, None] * inv
    ang_c = jnp.asarray(col)[:, None] * inv
    cr, sr, cc, sc = jnp.cos(ang_r), jnp.sin(ang_r), jnp.cos(ang_c), jnp.sin(ang_c)
    zero = jnp.zeros_like(sr)
    cos = jnp.concatenate([cr, cr, cc, cc], axis=-1)
    sin_a = jnp.concatenate([-sr, zero, -sc, zero], axis=-1)
    sin_b = jnp.concatenate([zero, sr, zero, sc], axis=-1)
    return tuple(jnp.tile(t, (1, N_Q_HEADS)) for t in (cos, sin_a, sin_b))


def _head_mean_matrix():
    blk = np.kron(np.eye(N_Q_HEADS, dtype=np.float32), np.full((HEAD_DIM, HEAD_DIM), 1.0 / HEAD_DIM, np.float32))
    return jnp.asarray(blk, BF16)


def _rows(m, b):
    return m[:b, None, :]


def kernel(x, c, ctx, c_ctx, norm_mix_g, norm_ffn_g, w_mod, b_mod, ev_w_in, ev_conv_w, ev_conv_b, ev_ln_g, ev_ln_b, ev_q_norm_g, ev_k_norm_g, ev_w_out, od_w_in, od_conv_w, od_pool_w, od_pool_scale, od_w_out, w_router, w_gate, w_up, w_down):
    b, l, d = x.shape
    depth = w_mod.shape[0]
    last_even = ((depth - 1) // 2) * 2

    n_rows = -(-(b + 1) // 8) * 8
    c_rows = jnp.zeros((n_rows, d), F32).at[:b].set(c).at[b].set(c_ctx)
    mods = _modulations(c_rows, w_mod, b_mod)
    rope = _rope_tables(l)
    bd = _head_mean_matrix()

    for i in range(depth):
        j = i // 2
        is_even = i % 2 == 0
        ctx_live = i < last_even
        m6 = mods[i].reshape(n_rows, 6, d)
        sh1, sc1, g1, sh2, sc2, g2 = [_rows(m6[:, t], b) for t in range(6)]
        mc = [jnp.broadcast_to(m6[b, t][None, None, :], (b, 1, d)) for t in range(6)]
        sh1c, sc1c, g1c, sh2c, sc2c, g2c = mc
        nm_g = norm_mix_g[i][None, :]
        nf_g = norm_ffn_g[i][None, :]
        wr_hi, wr_lo = _split_bf16(jnp.pad(w_router[i], ((0, 0), (0, LANES - N_EXPERTS))))
        wr_lo = jnp.concatenate([wr_hi, wr_lo], axis=1)

        if is_even:
            w_in = ev_w_in[j].astype(BF16)
            wo = ev_w_out[j].astype(BF16)
            qg = jnp.tile(ev_q_norm_g[j], N_Q_HEADS)[None, :]
            kg = jnp.tile(ev_k_norm_g[j], N_KV_HEADS)[None, :]
            conv_args = (ev_conv_w[j], ev_conv_b[j][None, :], ev_ln_g[j][None, :], ev_ln_b[j][None, :])
            u, q, k, v = _even_in(x, sh1, sc1, nm_g, w_in, qg, kg, bd, rope)
            uc, qc, kc, vc = _even_in(ctx, sh1c, sc1c, nm_g, w_in, qg, kg, bd, None)
            attn = _attention(q, jnp.concatenate([kc, k], axis=1), jnp.concatenate([vc, v], axis=1))
            conv = _conf_conv(u, *conv_args)
            x, h2, aff = _even_out(conv, attn, wo, x, g1, nf_g, sh2, sc2, wr_hi, wr_lo)
            if ctx_live:
                attn_c = _attention(qc, kc, vc)
                conv_c = _conf_conv(uc, *conv_args)
                ctx, h2c, affc = _even_out(conv_c, attn_c, wo, ctx, g1c, nf_g, sh2c, sc2c, wr_hi, wr_lo)
        else:
            w_in = od_w_in[j].astype(BF16)
            wo = od_w_out[j].astype(BF16)
            mix_args = (od_conv_w[j], od_pool_w[j].astype(BF16), od_pool_scale[j][None, :], wo)
            z = _odd_in(x, sh1, sc1, nm_g, w_in)
            x, h2, aff = _odd_mix(z, *mix_args, x, g1, nf_g, sh2, sc2, wr_hi, wr_lo)
            if ctx_live:
                zc = _odd_in(ctx, sh1c, sc1c, nm_g, w_in)
                ctx, h2c, affc = _odd_mix(zc, *mix_args, ctx, g1c, nf_g, sh2c, sc2c, wr_hi, wr_lo)
        streams = [(x, h2, aff, g2)] + ([(ctx, h2c, affc, g2c)] if ctx_live else [])
        outs = _moe(streams, w_gate, w_up, w_down, i)
        x = outs[0]
        if ctx_live:
            ctx = outs[1]
    return x
```

```python
import functools
import math

import jax
import jax.numpy as jnp
import numpy as np
from jax import lax
from jax.experimental import pallas as pl
from jax.experimental.pallas import tpu as pltpu

F32 = jnp.float32
BF16 = jnp.bfloat16

GRID_W = 64
N_Q_HEADS = 8
N_KV_HEADS = 2
HEAD_DIM = 64
D_ATTN = N_Q_HEADS * HEAD_DIM
D_KV = N_KV_HEADS * HEAD_DIM
ROPE_THETA = 10000.0
D_CONF = 512
D_SHORT = 512
D_POOL = 512
POOL_WINDOWS = (2, 4, 8, 16)
POOL_GROUP = D_POOL // len(POOL_WINDOWS)
N_EXPERTS = 16
CAPACITY_FACTOR = 2
EPS = 1e-6

LANES = 128
HALO = 16
VMEM_LIMIT = 56 * 1024 * 1024


def _sigmoid(x):
    return 1.0 / (1.0 + jnp.exp(-x))


def _dot(a, b):
    return jnp.dot(a, b, preferred_element_type=F32)


def _split_bf16(x):
    hi = x.astype(BF16)
    lo = (x - hi.astype(F32)).astype(BF16)
    return hi, lo


def _cparams(*sem):
    return pltpu.CompilerParams(dimension_semantics=sem, vmem_limit_bytes=VMEM_LIMIT)


def _mod_kernel(c_ref, w_ref, b_ref, o_ref):
    c = c_ref[...]
    s_hi, s_lo = _split_bf16(c * _sigmoid(c))
    w_hi, w_lo = _split_bf16(w_ref[0])
    o_ref[0] = _dot(s_hi, w_hi) + _dot(s_hi, w_lo) + _dot(s_lo, w_hi) + b_ref[0]


def _modulations(c_rows, w_mod, b_mod):
    depth, d, n = w_mod.shape
    r = c_rows.shape[0]
    tn = 1536
    return pl.pallas_call(
        _mod_kernel,
        grid=(depth, n // tn),
        in_specs=[pl.BlockSpec((r, d), lambda i, j: (0, 0)),
                  pl.BlockSpec((1, d, tn), lambda i, j: (i, 0, j)),
                  pl.BlockSpec((1, 1, tn), lambda i, j: (i, 0, j))],
        out_specs=pl.BlockSpec((1, r, tn), lambda i, j: (i, 0, j)),
        out_shape=jax.ShapeDtypeStruct((depth, r, n), F32),
        compiler_params=_cparams("arbitrary", "arbitrary"),
        name="modulations",
    )(c_rows, w_mod, b_mod.reshape(depth, 1, n))


def _modulated_norm(x, g, sh, sc):
    ms = jnp.mean(x * x, axis=-1, keepdims=True)
    return x * lax.rsqrt(ms + EPS) * g * (1.0 + sc) + sh


def _store_token_major(ref3, val):
    rows, d = val.shape
    nk = d // LANES
    for k in range(nk):
        ref3[0, pl.ds(k, rows, stride=nk), :] = val[:, k * LANES:(k + 1) * LANES]


def _post(x, y, g1, nf_g, sh2, sc2, wr_hi, wr_hilo, xo_ref, h3_ref, aff_ref):
    xn = x + g1 * y
    xo_ref[0] = xn
    h = _modulated_norm(xn, nf_g, sh2, sc2)
    h_hi, h_lo = _split_bf16(h)
    _store_token_major(h3_ref, h)
    both = _dot(h_hi, wr_hilo)
    lg = both[:, :LANES] + both[:, LANES:] + _dot(h_lo, wr_hi)
    lgt = lg.T[:N_EXPERTS]
    ex = jnp.exp(lgt - jnp.max(lgt, axis=0, keepdims=True))
    aff_ref[0] = ex / jnp.sum(ex, axis=0, keepdims=True)


def _head_rms(t, gain, bd):
    hi, lo = _split_bf16(t * t)
    ms = _dot(hi, bd) + _dot(lo, bd)
    return t * lax.rsqrt(ms + EPS) * gain


def _rope(t, cos, sin_a, sin_b):
    w = t.shape[-1]
    q = HEAD_DIM // 4
    return t * cos + pltpu.roll(t, w - q, 1) * sin_a + pltpu.roll(t, q, 1) * sin_b


def _even_in_kernel(*refs, rope):
    if rope:
        (x_ref, sh_ref, sc_ref, g_ref, w_ref, qg_ref, kg_ref, bd_ref, cos_ref, sa_ref, sb_ref,
         u_ref, q_ref, k_ref, v_ref) = refs
    else:
        x_ref, sh_ref, sc_ref, g_ref, w_ref, qg_ref, kg_ref, bd_ref, u_ref, q_ref, k_ref, v_ref = refs
    h = _modulated_norm(x_ref[0], g_ref[...], sh_ref[0], sc_ref[0])
    p = _dot(h.astype(BF16), w_ref[...])
    kv0 = 2 * D_CONF + D_ATTN
    u_ref[0] = p[:, :D_CONF] * _sigmoid(p[:, D_CONF:2 * D_CONF])
    q = _head_rms(p[:, 2 * D_CONF:kv0], qg_ref[...], bd_ref[...])
    k = _head_rms(p[:, kv0:kv0 + D_KV], kg_ref[...], bd_ref[:D_KV, :D_KV])
    if rope:
        cos, sa, sb = cos_ref[...], sa_ref[...], sb_ref[...]
        q = _rope(q, cos, sa, sb)
        k = _rope(k, cos[:, :D_KV], sa[:, :D_KV], sb[:, :D_KV])
    q_ref[0] = (q * (HEAD_DIM ** -0.5)).astype(BF16)
    k_ref[0] = k.astype(BF16)
    v_ref[0] = p[:, kv0 + D_KV:].astype(BF16)


def _even_in(x, sh, sc, g, w_bf, qg, kg, bd, rope_tabs):
    b, l, d = x.shape
    tm = min(512, l)
    n_in = w_bf.shape[1]
    row = lambda bi, t: (bi, 0, 0)
    tok = lambda bi, t: (bi, t, 0)
    const = lambda bi, t: (0, 0)
    in_specs = [pl.BlockSpec((1, tm, d), tok), pl.BlockSpec((1, 1, d), row), pl.BlockSpec((1, 1, d), row),
                pl.BlockSpec((1, d), const), pl.BlockSpec((d, n_in), const),
                pl.BlockSpec((1, D_ATTN), const), pl.BlockSpec((1, D_KV), const),
                pl.BlockSpec((D_ATTN, D_ATTN), const)]
    args = [x, sh, sc, g, w_bf, qg, kg, bd]
    if rope_tabs is not None:
        in_specs += [pl.BlockSpec((tm, D_ATTN), lambda bi, t: (t, 0))] * 3
        args += list(rope_tabs)
    return pl.pallas_call(
        functools.partial(_even_in_kernel, rope=rope_tabs is not None),
        grid=(b, l // tm),
        in_specs=in_specs,
        out_specs=[pl.BlockSpec((1, tm, D_CONF), tok), pl.BlockSpec((1, tm, D_ATTN), tok),
                   pl.BlockSpec((1, tm, D_KV), tok), pl.BlockSpec((1, tm, D_KV), tok)],
        out_shape=[jax.ShapeDtypeStruct((b, l, D_CONF), F32), jax.ShapeDtypeStruct((b, l, D_ATTN), BF16),
                   jax.ShapeDtypeStruct((b, l, D_KV), BF16), jax.ShapeDtypeStruct((b, l, D_KV), BF16)],
        compiler_params=_cparams("arbitrary", "arbitrary"),
        name="even_in_proj",
    )(*args)


def _fill_halo(xs_ref, prev_ref, cur_ref, next_ref, tm, nt):
    t = pl.program_id(1)
    xs_ref[0:HALO] = jnp.where(t > 0, prev_ref[0], 0.0)
    xs_ref[HALO:HALO + tm] = cur_ref[0]
    xs_ref[HALO + tm:HALO + tm + HALO] = jnp.where(t < nt - 1, next_ref[0], 0.0)


def _conf_conv_kernel(up_ref, uc_ref, un_ref, w_ref, b_ref, lg_ref, lb_ref, o_ref, xs_ref, sh_ref, *, tm, nt, rc):
    _fill_halo(xs_ref, up_ref, uc_ref, un_ref, tm, nt)
    width = w_ref.shape[0]
    pad = width // 2
    span = sh_ref.shape[1]
    for s in range(1, 8):
        sh_ref[s - 1] = xs_ref[pl.ds(s, span), :]
    for r in range(tm // rc):
        acc = jnp.zeros((rc, D_CONF), F32)
        for k in range(width):
            a, s = divmod(HALO - pad + k, 8)
            rows = pl.ds(r * rc + 8 * a, rc)
            win = xs_ref[rows, :] if s == 0 else sh_ref[s - 1, rows, :]
            acc = acc + win * w_ref[k:k + 1, :]
        u = acc + b_ref[...]
        mu = jnp.mean(u, axis=-1, keepdims=True)
        ctr = u - mu
        var = jnp.mean(ctr * ctr, axis=-1, keepdims=True)
        un = ctr * lax.rsqrt(var + EPS) * lg_ref[...] + lb_ref[...]
        o_ref[0, r * rc:(r + 1) * rc, :] = (un * _sigmoid(un)).astype(BF16)


def _halo_specs(tm, l, c):
    hb = tm // HALO
    last = l // HALO - 1
    return [pl.BlockSpec((1, HALO, c), lambda bi, t: (bi, jnp.maximum(t * hb - 1, 0), 0)),
            pl.BlockSpec((1, tm, c), lambda bi, t: (bi, t, 0)),
            pl.BlockSpec((1, HALO, c), lambda bi, t: (bi, jnp.minimum((t + 1) * hb, last), 0))]


def _conf_conv(u, conv_w, conv_b, ln_g, ln_b):
    b, l, c = u.shape
    tm = min(256, l)
    nt = l // tm
    const = lambda bi, t: (0, 0)
    return pl.pallas_call(
        functools.partial(_conf_conv_kernel, tm=tm, nt=nt, rc=256),
        grid=(b, nt),
        in_specs=_halo_specs(tm, l, c) + [pl.BlockSpec(conv_w.shape, const)] + [pl.BlockSpec((1, c), const)] * 3,
        out_specs=pl.BlockSpec((1, tm, c), lambda bi, t: (bi, t, 0)),
        out_shape=jax.ShapeDtypeStruct((b, l, c), BF16),
        scratch_shapes=[pltpu.VMEM((tm + 2 * HALO, c), F32), pltpu.VMEM((7, tm + 2 * HALO - 8, c), F32)],
        compiler_params=_cparams("arbitrary", "arbitrary"),
        name="conformer_conv",
    )(u, u, u, conv_w, conv_b, ln_g, ln_b)


def _attn_kernel(q_ref, kt_ref, v_ref, o_ref, *, chunks):
    q = q_ref[0]
    tq = q.shape[0]
    halves = []
    for half in range(2):
        m = jnp.full((tq, 1), -jnp.inf, F32)
        acc = jnp.zeros((tq, LANES), F32)
        for c0, cs in chunks:
            s = _dot(q, kt_ref[0, 0, half, :, c0:c0 + cs])
            m_new = jnp.maximum(m, jnp.max(s, axis=1, keepdims=True))
            p = jnp.exp(s - m_new)
            acc = jnp.exp(m - m_new) * acc + _dot(p.astype(BF16), v_ref[0, 0, half, c0:c0 + cs, :])
            m = m_new
        den = (1 - half) * HEAD_DIM
        halves.append(acc / acc[:, den:den + 1])
    lane = lax.broadcasted_iota(jnp.int32, (tq, LANES), 1)
    o_ref[0] = jnp.where(lane < HEAD_DIM, halves[0], halves[1]).astype(BF16)


def _kv_layouts(k, v):
    b, lk, _ = k.shape
    kt = k.reshape(b, lk, N_KV_HEADS, HEAD_DIM).transpose(0, 2, 3, 1)
    z = jnp.zeros_like(kt)
    kt = jnp.stack([jnp.concatenate([kt, z], axis=2), jnp.concatenate([z, kt], axis=2)], axis=2)
    vh = v.reshape(b, lk, N_KV_HEADS, HEAD_DIM).transpose(0, 2, 1, 3)
    pad = jnp.zeros_like(vh).at[..., 0].set(1.0)
    vv = jnp.stack([jnp.concatenate([vh, pad], axis=3), jnp.concatenate([pad, vh], axis=3)], axis=2)
    return kt, vv


def _key_chunks(lk, size=1024):
    head = lk % size
    chunks = [(0, head)] if head else []
    return tuple(chunks + [(c, size) for c in range(head, lk, size)])


def _attention(q, k, v):
    b, l, _ = q.shape
    lk = k.shape[1]
    kt, vv = _kv_layouts(k, v)
    tq = min(1024, l)
    n_pairs = D_ATTN // LANES
    pairs_per_kv = n_pairs // N_KV_HEADS
    return pl.pallas_call(
        functools.partial(_attn_kernel, chunks=_key_chunks(lk)),
        grid=(b, n_pairs, l // tq),
        in_specs=[pl.BlockSpec((1, tq, LANES), lambda bi, j, t: (bi, t, j)),
                  pl.BlockSpec((1, 1, 2, LANES, lk), lambda bi, j, t: (bi, j // pairs_per_kv, 0, 0, 0)),
                  pl.BlockSpec((1, 1, 2, lk, LANES), lambda bi, j, t: (bi, j // pairs_per_kv, 0, 0, 0))],
        out_specs=pl.BlockSpec((1, tq, LANES), lambda bi, j, t: (bi, t, j)),
        out_shape=jax.ShapeDtypeStruct((b, l, D_ATTN), BF16),
        compiler_params=_cparams("arbitrary", "arbitrary", "arbitrary"),
        name="attention",
    )(q, kt, vv)


def _even_out_kernel(cv_ref, at_ref, wo_ref, x_ref, g1_ref, nfg_ref, sh2_ref, sc2_ref, wrh_ref, wrl_ref,
                     xo_ref, h2_ref, aff_ref):
    y = _dot(cv_ref[0], wo_ref[:D_CONF, :]) + _dot(at_ref[0], wo_ref[D_CONF:, :])
    _post(x_ref[0], y, g1_ref[0], nfg_ref[...], sh2_ref[0], sc2_ref[0], wrh_ref[...], wrl_ref[...],
          xo_ref, h2_ref, aff_ref)


def _post_specs(tm, d):
    row = lambda bi, t: (bi, 0, 0)
    const = lambda bi, t: (0, 0)
    in_specs = [pl.BlockSpec((1, tm, d), lambda bi, t: (bi, t, 0)), pl.BlockSpec((1, 1, d), row),
                pl.BlockSpec((1, d), const), pl.BlockSpec((1, 1, d), row), pl.BlockSpec((1, 1, d), row),
                pl.BlockSpec((d, LANES), const), pl.BlockSpec((d, 2 * LANES), const)]
    nk = d // LANES
    out_specs = [pl.BlockSpec((1, tm, d), lambda bi, t: (bi, t, 0)),
                 pl.BlockSpec((1, tm * nk, LANES), lambda bi, t: (bi, t, 0)),
                 pl.BlockSpec((1, N_EXPERTS, tm), lambda bi, t: (bi, 0, t))]
    return in_specs, out_specs


def _post_out_shapes(b, l, d):
    return [jax.ShapeDtypeStruct((b, l, d), F32), jax.ShapeDtypeStruct((b, l * (d // LANES), LANES), F32),
            jax.ShapeDtypeStruct((b, N_EXPERTS, l), F32)]


def _even_out(conv, attn, wo_bf, x, g1, nf_g, sh2, sc2, wr_hi, wr_lo):
    b, l, d = x.shape
    tm = min(512, l)
    tok = lambda bi, t: (bi, t, 0)
    post_in, post_out = _post_specs(tm, d)
    return pl.pallas_call(
        _even_out_kernel,
        grid=(b, l // tm),
        in_specs=[pl.BlockSpec((1, tm, D_CONF), tok), pl.BlockSpec((1, tm, D_ATTN), tok),
                  pl.BlockSpec(wo_bf.shape, lambda bi, t: (0, 0))] + post_in,
        out_specs=post_out,
        out_shape=_post_out_shapes(b, l, d),
        compiler_params=_cparams("arbitrary", "arbitrary"),
        name="even_out_proj",
    )(conv, attn, wo_bf, x, g1, nf_g, sh2, sc2, wr_hi, wr_lo)


def _odd_in_kernel(x_ref, sh_ref, sc_ref, g_ref, w_ref, z_ref):
    h = _modulated_norm(x_ref[0], g_ref[...], sh_ref[0], sc_ref[0])
    p = _dot(h.astype(BF16), w_ref[...])
    ds_ = D_SHORT
    z_ref[0, :, :ds_] = p[:, 2 * ds_:3 * ds_] * p[:, :ds_]
    z_ref[0, :, ds_:2 * ds_] = p[:, 3 * ds_:]
    z_ref[0, :, 2 * ds_:] = p[:, ds_:2 * ds_]


def _odd_in(x, sh, sc, g, w_bf):
    b, l, d = x.shape
    tm = min(512, l)
    row = lambda bi, t: (bi, 0, 0)
    tok = lambda bi, t: (bi, t, 0)
    const = lambda bi, t: (0, 0)
    nz = 2 * D_SHORT + D_POOL
    return pl.pallas_call(
        _odd_in_kernel,
        grid=(b, l // tm),
        in_specs=[pl.BlockSpec((1, tm, d), tok), pl.BlockSpec((1, 1, d), row), pl.BlockSpec((1, 1, d), row),
                  pl.BlockSpec((1, d), const), pl.BlockSpec(w_bf.shape, const)],
        out_specs=pl.BlockSpec((1, tm, nz), tok),
        out_shape=jax.ShapeDtypeStruct((b, l, nz), F32),
        compiler_params=_cparams("arbitrary", "arbitrary"),
        name="odd_in_proj",
    )(x, sh, sc, g, w_bf)


def _odd_mix_kernel(zp_ref, zc_ref, zn_ref, cw_ref, pw_ref, ps_ref, wo_ref,
                    x_ref, g1_ref, nfg_ref, sh2_ref, sc2_ref, wrh_ref, wrl_ref,
                    xo_ref, h2_ref, aff_ref, xs_ref, *, tm, nt, n_tok):
    _fill_halo(xs_ref, zp_ref, zc_ref, zn_ref, tm, nt)
    ds_ = D_SHORT
    width = cw_ref.shape[0]
    pad = width // 2
    conv = jnp.zeros((tm, ds_), F32)
    for k in range(width):
        conv = conv + xs_ref[pl.ds(HALO - pad + k, tm), 0:ds_] * cw_ref[k:k + 1, :]
    short = xs_ref[pl.ds(HALO, tm), 2 * ds_:3 * ds_] * conv
    pos = pl.program_id(1) * tm + lax.broadcasted_iota(jnp.int32, (tm, POOL_GROUP), 0)
    pooled = []
    for gi, w in enumerate(POOL_WINDOWS):
        c0 = ds_ + gi * POOL_GROUP
        tot = jnp.zeros((tm, POOL_GROUP), F32)
        for dlt in range(-(w // 2), w // 2):
            tot = tot + xs_ref[pl.ds(HALO + dlt, tm), c0:c0 + POOL_GROUP]
        cnt = (jnp.minimum(pos + w // 2, n_tok) - jnp.maximum(pos - w // 2, 0)).astype(F32)
        diff = tot / cnt - xs_ref[pl.ds(HALO, tm), c0:c0 + POOL_GROUP]
        pooled.append(_dot(diff.astype(BF16), pw_ref[gi]))
    pool = jnp.concatenate(pooled, axis=-1) * ps_ref[...]
    y = _dot(short.astype(BF16), wo_ref[:ds_, :]) + _dot(pool.astype(BF16), wo_ref[ds_:, :])
    _post(x_ref[0], y, g1_ref[0], nfg_ref[...], sh2_ref[0], sc2_ref[0], wrh_ref[...], wrl_ref[...],
          xo_ref, h2_ref, aff_ref)


def _odd_mix(z, conv_w, pool_w_bf, pool_scale, wo_bf, x, g1, nf_g, sh2, sc2, wr_hi, wr_lo):
    b, l, d = x.shape
    tm = min(256, l)
    nt = l // tm
    nz = z.shape[-1]
    const = lambda bi, t: (0, 0)
    post_in, post_out = _post_specs(tm, d)
    return pl.pallas_call(
        functools.partial(_odd_mix_kernel, tm=tm, nt=nt, n_tok=l),
        grid=(b, nt),
        in_specs=_halo_specs(tm, l, nz) + [pl.BlockSpec(conv_w.shape, const),
                                            pl.BlockSpec(pool_w_bf.shape, lambda bi, t: (0, 0, 0)),
                                            pl.BlockSpec((1, D_POOL), const),
                                            pl.BlockSpec(wo_bf.shape, const)] + post_in,
        out_specs=post_out,
        out_shape=_post_out_shapes(b, l, d),
        scratch_shapes=[pltpu.VMEM((tm + 2 * HALO, nz), F32)],
        compiler_params=_cparams("arbitrary", "arbitrary"),
        name="odd_mixer",
    )(z, z, z, conv_w, pool_w_bf, pool_scale, wo_bf, x, g1, nf_g, sh2, sc2, wr_hi, wr_lo)


def _route_kernel(aff_ref, tri_ref, pos_ref, *, cap):
    key = lax.bitcast_convert_type(aff_ref[0], jnp.int32)
    n = key.shape[1]
    capf = float(cap)
    thr = jnp.zeros((N_EXPERTS, 1), jnp.int32)
    for bit in range(30, -1, -1):
        cand = thr | (1 << bit)
        cnt = jnp.sum(jnp.where(key >= cand, 1.0, 0.0), axis=1, keepdims=True)
        thr = jnp.where(cnt >= capf, cand, thr)
    n_gt = jnp.sum(jnp.where(key > thr, 1.0, 0.0), axis=1, keepdims=True)
    need = capf - n_gt
    tri = tri_ref[...]
    off_eq = jnp.zeros((N_EXPERTS, 1), F32)
    off_sel = jnp.zeros((N_EXPERTS, 1), F32)
    for j in range(n // LANES):
        kj = key[:, j * LANES:(j + 1) * LANES]
        gt = kj > thr
        eqf = jnp.where(kj == thr, 1.0, 0.0)
        rank = _dot(eqf.astype(BF16), tri) + off_eq - eqf
        self_ = jnp.where(gt, 1.0, jnp.where(rank < need, eqf, 0.0))
        slot = _dot(self_.astype(BF16), tri) + off_sel - 1.0
        pos_ref[0, :, j * LANES:(j + 1) * LANES] = jnp.where(self_ > 0.0, slot, -1.0)
        off_eq = off_eq + jnp.sum(eqf, axis=1, keepdims=True)
        off_sel = off_sel + jnp.sum(self_, axis=1, keepdims=True)


def _route(aff_t, cap):
    b, e, n = aff_t.shape
    tri = jnp.asarray(np.triu(np.ones((LANES, LANES), np.float32)), BF16)
    return pl.pallas_call(
        functools.partial(_route_kernel, cap=cap),
        grid=(b,),
        in_specs=[pl.BlockSpec((1, e, n), lambda bi: (bi, 0, 0)), pl.BlockSpec((LANES, LANES), lambda bi: (0, 0))],
        out_specs=pl.BlockSpec((1, e, n), lambda bi: (bi, 0, 0)),
        out_shape=jax.ShapeDtypeStruct((b, e, n), F32),
        compiler_params=_cparams("arbitrary"),
        name="route",
    )(aff_t, tri)


ROWS_PER_STEP = 1024


def _experts_per_step(cap):
    return max(1, min(N_EXPERTS, ROWS_PER_STEP // cap))


def _slot_index_kernel(pos_ref, idx_ref, *, cap, sc, eps):
    n = pos_ref.shape[3]
    for ee in range(eps):
        pos = pos_ref[0, ee]
        for c in range(cap // sc):
            slot = (lax.broadcasted_iota(jnp.int32, (sc, LANES), 0) + c * sc).astype(F32)
            tok_acc = jnp.zeros((sc, LANES), F32)
            for j in range(n // LANES):
                tok = (lax.broadcasted_iota(jnp.int32, (1, LANES), 1) + j * LANES).astype(F32)
                tok_acc = tok_acc + jnp.where(slot == pos[:, j * LANES:(j + 1) * LANES], tok, 0.0)
            idx_ref[0, ee, :, c * sc:(c + 1) * sc] = jnp.sum(tok_acc.T, axis=0, keepdims=True).astype(jnp.int32)


def _slot_index(pos, cap):
    b, e, n = pos.shape
    sc = min(LANES, cap)
    eps = _experts_per_step(cap)
    sel = lambda bi, ei: (bi, ei, 0, 0)
    return pl.pallas_call(
        functools.partial(_slot_index_kernel, cap=cap, sc=sc, eps=eps),
        grid=(b, e // eps),
        in_specs=[pl.BlockSpec((1, eps, 1, n), sel)],
        out_specs=pl.BlockSpec((1, eps, 1, cap), sel),
        out_shape=jax.ShapeDtypeStruct((b, e, 1, cap), jnp.int32),
        compiler_params=_cparams("arbitrary", "arbitrary"),
        name="slot_index",
    )(pos.reshape(b, e, 1, n))


def _gather_kernel(idx_ref, h3_ref, *rest, cap, nk, stride, eps):
    xe_ref, tile_ref = rest[-2:]
    for ee in range(eps):
        for s in range(cap):
            t = idx_ref[0, ee, 0, s]
            tile_ref[ee, pl.ds(s, nk, stride=stride), :] = h3_ref[0, pl.ds(pl.multiple_of(t * nk, nk), nk), :]
        for k in range(nk):
            xe_ref[ee, :, k * LANES:(k + 1) * LANES] = tile_ref[ee, k * stride:k * stride + cap, :].astype(BF16)


def _gather(idx, h3, cap, total_rows, row0, xe_all=None):
    b, e = idx.shape[:2]
    rows = h3.shape[1]
    nk = 8
    d = nk * LANES
    stride = cap + 8
    eps = _experts_per_step(cap)
    blk0 = row0 // cap
    in_specs = [pl.BlockSpec((1, eps, 1, cap), lambda bi, ei: (bi, ei, 0, 0), memory_space=pltpu.SMEM),
                pl.BlockSpec((1, rows, LANES), lambda bi, ei: (bi, 0, 0))]
    args = [idx, h3]
    aliases = {}
    if xe_all is not None:
        in_specs.append(pl.BlockSpec(memory_space=pl.ANY))
        args.append(xe_all)
        aliases = {2: 0}
    return pl.pallas_call(
        functools.partial(_gather_kernel, cap=cap, nk=nk, stride=stride, eps=eps),
        grid=(b, e // eps),
        in_specs=in_specs,
        out_specs=pl.BlockSpec((eps, cap, d), lambda bi, ei: (ei, blk0 + bi, 0)),
        out_shape=jax.ShapeDtypeStruct((e, total_rows, d), BF16),
        scratch_shapes=[pltpu.VMEM((eps, nk * stride, LANES), F32)],
        input_output_aliases=aliases,
        compiler_params=_cparams("arbitrary", "arbitrary"),
        name="moe_gather",
    )(*args)


def _ffn_kernel(xe_ref, wg_ref, wu_ref, wd_ref, ye_ref, acc_ref, wg_bf, wu_bf, wd_bf, *, nf, tm):
    f = pl.program_id(1)
    m = pl.program_id(2)

    @pl.when(m == 0)
    def _():
        wg_bf[...] = wg_ref[0, 0].astype(BF16)
        wu_bf[...] = wu_ref[0, 0].astype(BF16)
        wd_bf[...] = wd_ref[0, 0].astype(BF16)

    rows = pl.ds(pl.multiple_of(m * tm, tm), tm)

    def step(first, last):
        x = xe_ref[0]
        g = _dot(x, wg_bf[...])
        u = _dot(x, wu_bf[...])
        tot = _dot((g * _sigmoid(g) * u).astype(BF16), wd_bf[...])
        if not first:
            tot = acc_ref[rows, :] + tot
        if last:
            _store_token_major(ye_ref, tot)
        else:
            acc_ref[rows, :] = tot

    pl.when(f == 0)(lambda: step(True, False))
    pl.when(jnp.logical_and(f > 0, f < nf - 1))(lambda: step(False, False))
    pl.when(f == nf - 1)(lambda: step(False, True))


def _expert_ffn(xe, w_gate, w_up, w_down, layer):
    e, m, d = xe.shape
    dff = w_gate.shape[-1]
    nk = d // LANES
    tm = max(t for t in range(16, min(m, 1152) + 1, 16) if m % t == 0)
    tf = 512
    nf = dff // tf
    out_tile = lambda ei, f, mi: (ei, jnp.where(f == nf - 1, mi, 0), 0)
    return pl.pallas_call(
        functools.partial(_ffn_kernel, nf=nf, tm=tm),
        grid=(e, nf, m // tm),
        in_specs=[pl.BlockSpec((1, tm, d), lambda ei, f, mi: (ei, mi, 0)),
                  pl.BlockSpec((1, 1, d, tf), lambda ei, f, mi: (layer, ei, 0, f)),
                  pl.BlockSpec((1, 1, d, tf), lambda ei, f, mi: (layer, ei, 0, f)),
                  pl.BlockSpec((1, 1, tf, d), lambda ei, f, mi: (layer, ei, f, 0))],
        out_specs=pl.BlockSpec((1, tm * nk, LANES), out_tile),
        out_shape=jax.ShapeDtypeStruct((e, m * nk, LANES), F32),
        scratch_shapes=[pltpu.VMEM((m, d), F32), pltpu.VMEM((d, tf), BF16), pltpu.VMEM((d, tf), BF16),
                        pltpu.VMEM((tf, d), BF16)],
        compiler_params=_cparams("arbitrary", "arbitrary", "arbitrary"),
        name="expert_ffn",
    )(xe, w_gate, w_up, w_down)


def _combine_kernel(idx_ref, aff_ref, ye_ref, x_ref, g2_ref, o_ref, acc_ref, *, cap, nk, tm, batch, eps, n_scatter):
    k = pl.program_id(1)

    @pl.when(k == 0)
    def _():
        acc_ref[...] = jnp.zeros_like(acc_ref)

    @pl.when(k < n_scatter)
    def _():
        for ee in range(eps):
            for s0 in range(0, cap, batch):
                rows = []
                for s in range(s0, s0 + batch):
                    t = idx_ref[0, ee, 0, s]
                    start = pl.multiple_of(t * nk, nk)
                    rows.append((start, acc_ref[pl.ds(start, nk), :]
                                 + aff_ref[0, ee, 0, t] * ye_ref[ee, s * nk:(s + 1) * nk, :]))
                for start, val in rows:
                    acc_ref[pl.ds(start, nk), :] = val

    @pl.when(k >= n_scatter)
    def _():
        view = acc_ref.at[pl.ds(pl.multiple_of((k - n_scatter) * (tm * nk), tm * nk), tm * nk), :]
        for c in range(nk):
            lanes = slice(c * LANES, (c + 1) * LANES)
            o_ref[0, :, lanes] = x_ref[0, :, lanes] + g2_ref[0, :, lanes] * view[pl.ds(c, tm, stride=nk), :]


def _combine(idx, aff_t, ye3, x, g2, cap, row0):
    b, n, d = x.shape
    e = idx.shape[1]
    nk = d // LANES
    tm = min(1024, n)
    nt = n // tm
    eps = _experts_per_step(cap)
    ns = e // eps
    blk0 = row0 // cap
    group = lambda bi, k: (bi, jnp.minimum(k, ns - 1), 0, 0)
    smem = lambda width: pl.BlockSpec((1, eps, 1, width), group, memory_space=pltpu.SMEM)
    tile = lambda bi, k: (bi, jnp.maximum(k - ns, 0), 0)
    return pl.pallas_call(
        functools.partial(_combine_kernel, cap=cap, nk=nk, tm=tm, batch=8, eps=eps, n_scatter=ns),
        grid=(b, ns + nt),
        in_specs=[smem(cap), smem(n),
                  pl.BlockSpec((eps, cap * nk, LANES), lambda bi, k: (jnp.minimum(k, ns - 1), blk0 + bi, 0)),
                  pl.BlockSpec((1, tm, d), tile),
                  pl.BlockSpec((1, 1, d), lambda bi, k: (bi, 0, 0))],
        out_specs=pl.BlockSpec((1, tm, d), tile),
        out_shape=jax.ShapeDtypeStruct((b, n, d), F32),
        scratch_shapes=[pltpu.VMEM((n * nk, LANES), F32)],
        compiler_params=_cparams("arbitrary", "arbitrary"),
        name="moe_combine",
    )(idx, aff_t.reshape(b, e, 1, n), ye3, x, g2)


def _moe(streams, w_gate, w_up, w_down, layer):
    caps = [CAPACITY_FACTOR * x.shape[1] // N_EXPERTS for x, _, _, _ in streams]
    row0s = list(np.cumsum([0] + [x.shape[0] * cap for (x, _, _, _), cap in zip(streams, caps)]))
    total_rows = int(row0s[-1])
    idxs, xe = [], None
    for (x, h3, aff_t, _), cap, row0 in zip(streams, caps, row0s):
        idx = _slot_index(_route(aff_t, cap), cap)
        xe = _gather(idx, h3, cap, total_rows, int(row0), xe)
        idxs.append(idx)
    ye3 = _expert_ffn(xe, w_gate, w_up, w_down, layer)
    return [_combine(idx, aff_t, ye3, x, g2, cap, int(row0))
            for (x, _, aff_t, g2), idx, cap, row0 in zip(streams, idxs, caps, row0s)]


def _rope_tables(n_tok):
    rows = n_tok // GRID_W
    row = np.repeat(np.arange(rows), GRID_W).astype(np.float32)
    col = np.tile(np.arange(GRID_W), rows).astype(np.float32)
    n_freq = HEAD_DIM // 4
    inv = jnp.asarray(ROPE_THETA, F32) ** (-jnp.arange(n_freq, dtype=F32) / n_freq)
    ang_r = jnp.asarray(row)[:, None] * inv
    ang_c = jnp.asarray(col)[:, None] * inv
    cr, sr, cc, sc = jnp.cos(ang_r), jnp.sin(ang_r), jnp.cos(ang_c), jnp.sin(ang_c)
    zero = jnp.zeros_like(sr)
    cos = jnp.concatenate([cr, cr, cc, cc], axis=-1)
    sin_a = jnp.concatenate([-sr, zero, -sc, zero], axis=-1)
    sin_b = jnp.concatenate([zero, sr, zero, sc], axis=-1)
    return tuple(jnp.tile(t, (1, N_Q_HEADS)) for t in (cos, sin_a, sin_b))


def _head_mean_matrix():
    blk = np.kron(np.eye(N_Q_HEADS, dtype=np.float32), np.full((HEAD_DIM, HEAD_DIM), 1.0 / HEAD_DIM, np.float32))
    return jnp.asarray(blk, BF16)


def _rows(m, b):
    return m[:b, None, :]


def kernel(x, c, ctx, c_ctx, norm_mix_g, norm_ffn_g, w_mod, b_mod, ev_w_in, ev_conv_w, ev_conv_b, ev_ln_g, ev_ln_b, ev_q_norm_g, ev_k_norm_g, ev_w_out, od_w_in, od_conv_w, od_pool_w, od_pool_scale, od_w_out, w_router, w_gate, w_up, w_down):
    b, l, d = x.shape
    depth = w_mod.shape[0]
    last_even = ((depth - 1) // 2) * 2

    n_rows = -(-(b + 1) // 8) * 8
    c_rows = jnp.zeros((n_rows, d), F32).at[:b].set(c).at[b].set(c_ctx)
    mods = _modulations(c_rows, w_mod, b_mod)
    rope = _rope_tables(l)
    bd = _head_mean_matrix()

    for i in range(depth):
        j = i // 2
        is_even = i % 2 == 0
        ctx_live = i < last_even
        m6 = mods[i].reshape(n_rows, 6, d)
        sh1, sc1, g1, sh2, sc2, g2 = [_rows(m6[:, t], b) for t in range(6)]
        mc = [jnp.broadcast_to(m6[b, t][None, None, :], (b, 1, d)) for t in range(6)]
        sh1c, sc1c, g1c, sh2c, sc2c, g2c = mc
        nm_g = norm_mix_g[i][None, :]
        nf_g = norm_ffn_g[i][None, :]
        wr_hi, wr_lo = _split_bf16(jnp.pad(w_router[i], ((0, 0), (0, LANES - N_EXPERTS))))
        wr_lo = jnp.concatenate([wr_hi, wr_lo], axis=1)

        if is_even:
            w_in = ev_w_in[j].astype(BF16)
            wo = ev_w_out[j].astype(BF16)
            qg = jnp.tile(ev_q_norm_g[j], N_Q_HEADS)[None, :]
            kg = jnp.tile(ev_k_norm_g[j], N_KV_HEADS)[None, :]
            conv_args = (ev_conv_w[j], ev_conv_b[j][None, :], ev_ln_g[j][None, :], ev_ln_b[j][None, :])
            u, q, k, v = _even_in(x, sh1, sc1, nm_g, w_in, qg, kg, bd, rope)
            uc, qc, kc, vc = _even_in(ctx, sh1c, sc1c, nm_g, w_in, qg, kg, bd, None)
            attn = _attention(q, jnp.concatenate([kc, k], axis=1), jnp.concatenate([vc, v], axis=1))
            conv = _conf_conv(u, *conv_args)
            x, h2, aff = _even_out(conv, attn, wo, x, g1, nf_g, sh2, sc2, wr_hi, wr_lo)
            if ctx_live:
                attn_c = _attention(qc, kc, vc)
                conv_c = _conf_conv(uc, *conv_args)
                ctx, h2c, affc = _even_out(conv_c, attn_c, wo, ctx, g1c, nf_g, sh2c, sc2c, wr_hi, wr_lo)
        else:
            w_in = od_w_in[j].astype(BF16)
            wo = od_w_out[j].astype(BF16)
            mix_args = (od_conv_w[j], od_pool_w[j].astype(BF16), od_pool_scale[j][None, :], wo)
            z = _odd_in(x, sh1, sc1, nm_g, w_in)
            x, h2, aff = _odd_mix(z, *mix_args, x, g1, nf_g, sh2, sc2, wr_hi, wr_lo)
            if ctx_live:
                zc = _odd_in(ctx, sh1c, sc1c, nm_g, w_in)
                ctx, h2c, affc = _odd_mix(zc, *mix_args, ctx, g1c, nf_g, sh2c, sc2c, wr_hi, wr_lo)
        streams = [(x, h2, aff, g2)] + ([(ctx, h2c, affc, g2c)] if ctx_live else [])
        outs = _moe(streams, w_gate, w_up, w_down, i)
        x = outs[0]
        if ctx_live:
            ctx = outs[1]
    return x
```

```python
import functools
import math

import jax
import jax.numpy as jnp
import numpy as np
from jax import lax
from jax.experimental import pallas as pl
from jax.experimental.pallas import tpu as pltpu

F32 = jnp.float32
BF16 = jnp.bfloat16

GRID_W = 64
N_Q_HEADS = 8
N_KV_HEADS = 2
HEAD_DIM = 64
D_ATTN = N_Q_HEADS * HEAD_DIM
D_KV = N_KV_HEADS * HEAD_DIM
ROPE_THETA = 10000.0
D_CONF = 512
D_SHORT = 512
D_POOL = 512
POOL_WINDOWS = (2, 4, 8, 16)
POOL_GROUP = D_POOL // len(POOL_WINDOWS)
N_EXPERTS = 16
CAPACITY_FACTOR = 2
EPS = 1e-6

LANES = 128
HALO = 16
VMEM_LIMIT = 56 * 1024 * 1024


def _sigmoid(x):
    return 1.0 / (1.0 + jnp.exp(-x))


def _dot(a, b):
    return jnp.dot(a, b, preferred_element_type=F32)


def _split_bf16(x):
    hi = x.astype(BF16)
    lo = (x - hi.astype(F32)).astype(BF16)
    return hi, lo


def _cparams(*sem):
    return pltpu.CompilerParams(dimension_semantics=sem, vmem_limit_bytes=VMEM_LIMIT)


def _mod_kernel(c_ref, w_ref, b_ref, o_ref):
    c = c_ref[...]
    s_hi, s_lo = _split_bf16(c * _sigmoid(c))
    w_hi, w_lo = _split_bf16(w_ref[0])
    o_ref[0] = _dot(s_hi, w_hi) + _dot(s_hi, w_lo) + _dot(s_lo, w_hi) + b_ref[0]


def _modulations(c_rows, w_mod, b_mod):
    depth, d, n = w_mod.shape
    r = c_rows.shape[0]
    tn = 1536
    return pl.pallas_call(
        _mod_kernel,
        grid=(depth, n // tn),
        in_specs=[pl.BlockSpec((r, d), lambda i, j: (0, 0)),
                  pl.BlockSpec((1, d, tn), lambda i, j: (i, 0, j)),
                  pl.BlockSpec((1, 1, tn), lambda i, j: (i, 0, j))],
        out_specs=pl.BlockSpec((1, r, tn), lambda i, j: (i, 0, j)),
        out_shape=jax.ShapeDtypeStruct((depth, r, n), F32),
        compiler_params=_cparams("arbitrary", "arbitrary"),
        name="modulations",
    )(c_rows, w_mod, b_mod.reshape(depth, 1, n))


def _modulated_norm(x, g, sh, sc):
    ms = jnp.mean(x * x, axis=-1, keepdims=True)
    return x * lax.rsqrt(ms + EPS) * g * (1.0 + sc) + sh


def _store_token_major(ref3, val):
    rows, d = val.shape
    nk = d // LANES
    for k in range(nk):
        ref3[0, pl.ds(k, rows, stride=nk), :] = val[:, k * LANES:(k + 1) * LANES]


def _post(x, y, g1, nf_g, sh2, sc2, wr_hi, wr_hilo, xo_ref, h3_ref, aff_ref):
    xn = x + g1 * y
    xo_ref[0] = xn
    h = _modulated_norm(xn, nf_g, sh2, sc2)
    h_hi, h_lo = _split_bf16(h)
    _store_token_major(h3_ref, h)
    both = _dot(h_hi, wr_hilo)
    lg = both[:, :LANES] + both[:, LANES:] + _dot(h_lo, wr_hi)
    lgt = lg.T[:N_EXPERTS]
    ex = jnp.exp(lgt - jnp.max(lgt, axis=0, keepdims=True))
    aff_ref[0] = ex / jnp.sum(ex, axis=0, keepdims=True)


def _head_rms(t, gain, bd):
    hi, lo = _split_bf16(t * t)
    ms = _dot(hi, bd) + _dot(lo, bd)
    return t * lax.rsqrt(ms + EPS) * gain


def _rope(t, cos, sin_a, sin_b):
    w = t.shape[-1]
    q = HEAD_DIM // 4
    return t * cos + pltpu.roll(t, w - q, 1) * sin_a + pltpu.roll(t, q, 1) * sin_b


def _even_in_kernel(*refs, rope):
    if rope:
        (x_ref, sh_ref, sc_ref, g_ref, w_ref, qg_ref, kg_ref, bd_ref, cos_ref, sa_ref, sb_ref,
         u_ref, q_ref, k_ref, v_ref) = refs
    else:
        x_ref, sh_ref, sc_ref, g_ref, w_ref, qg_ref, kg_ref, bd_ref, u_ref, q_ref, k_ref, v_ref = refs
    h = _modulated_norm(x_ref[0], g_ref[...], sh_ref[0], sc_ref[0])
    p = _dot(h.astype(BF16), w_ref[...])
    kv0 = 2 * D_CONF + D_ATTN
    u_ref[0] = p[:, :D_CONF] * _sigmoid(p[:, D_CONF:2 * D_CONF])
    q = _head_rms(p[:, 2 * D_CONF:kv0], qg_ref[...], bd_ref[...])
    k = _head_rms(p[:, kv0:kv0 + D_KV], kg_ref[...], bd_ref[:D_KV, :D_KV])
    if rope:
        cos, sa, sb = cos_ref[...], sa_ref[...], sb_ref[...]
        q = _rope(q, cos, sa, sb)
        k = _rope(k, cos[:, :D_KV], sa[:, :D_KV], sb[:, :D_KV])
    q_ref[0] = (q * (HEAD_DIM ** -0.5)).astype(BF16)
    k_ref[0] = k.astype(BF16)
    v_ref[0] = p[:, kv0 + D_KV:].astype(BF16)


def _even_in(x, sh, sc, g, w_bf, qg, kg, bd, rope_tabs):
    b, l, d = x.shape
    tm = min(512, l)
    n_in = w_bf.shape[1]
    row = lambda bi, t: (bi, 0, 0)
    tok = lambda bi, t: (bi, t, 0)
    const = lambda bi, t: (0, 0)
    in_specs = [pl.BlockSpec((1, tm, d), tok), pl.BlockSpec((1, 1, d), row), pl.BlockSpec((1, 1, d), row),
                pl.BlockSpec((1, d), const), pl.BlockSpec((d, n_in), const),
                pl.BlockSpec((1, D_ATTN), const), pl.BlockSpec((1, D_KV), const),
                pl.BlockSpec((D_ATTN, D_ATTN), const)]
    args = [x, sh, sc, g, w_bf, qg, kg, bd]
    if rope_tabs is not None:
        in_specs += [pl.BlockSpec((tm, D_ATTN), lambda bi, t: (t, 0))] * 3
        args += list(rope_tabs)
    return pl.pallas_call(
        functools.partial(_even_in_kernel, rope=rope_tabs is not None),
        grid=(b, l // tm),
        in_specs=in_specs,
        out_specs=[pl.BlockSpec((1, tm, D_CONF), tok), pl.BlockSpec((1, tm, D_ATTN), tok),
                   pl.BlockSpec((1, tm, D_KV), tok), pl.BlockSpec((1, tm, D_KV), tok)],
        out_shape=[jax.ShapeDtypeStruct((b, l, D_CONF), F32), jax.ShapeDtypeStruct((b, l, D_ATTN), BF16),
                   jax.ShapeDtypeStruct((b, l, D_KV), BF16), jax.ShapeDtypeStruct((b, l, D_KV), BF16)],
        compiler_params=_cparams("arbitrary", "arbitrary"),
        name="even_in_proj",
    )(*args)


def _fill_halo(xs_ref, prev_ref, cur_ref, next_ref, tm, nt):
    t = pl.program_id(1)
    xs_ref[0:HALO] = jnp.where(t > 0, prev_ref[0], 0.0)
    xs_ref[HALO:HALO + tm] = cur_ref[0]
    xs_ref[HALO + tm:HALO + tm + HALO] = jnp.where(t < nt - 1, next_ref[0], 0.0)


def _conf_conv_kernel(up_ref, uc_ref, un_ref, w_ref, b_ref, lg_ref, lb_ref, o_ref, xs_ref, sh_ref, *, tm, nt, rc):
    _fill_halo(xs_ref, up_ref, uc_ref, un_ref, tm, nt)
    width = w_ref.shape[0]
    pad = width // 2
    span = sh_ref.shape[1]
    for s in range(1, 8):
        sh_ref[s - 1] = xs_ref[pl.ds(s, span), :]
    for r in range(tm // rc):
        acc = jnp.zeros((rc, D_CONF), F32)
        for k in range(width):
            a, s = divmod(HALO - pad + k, 8)
            rows = pl.ds(r * rc + 8 * a, rc)
            win = xs_ref[rows, :] if s == 0 else sh_ref[s - 1, rows, :]
            acc = acc + win * w_ref[k:k + 1, :]
        u = acc + b_ref[...]
        mu = jnp.mean(u, axis=-1, keepdims=True)
        ctr = u - mu
        var = jnp.mean(ctr * ctr, axis=-1, keepdims=True)
        un = ctr * lax.rsqrt(var + EPS) * lg_ref[...] + lb_ref[...]
        o_ref[0, r * rc:(r + 1) * rc, :] = (un * _sigmoid(un)).astype(BF16)


def _halo_specs(tm, l, c):
    hb = tm // HALO
    last = l // HALO - 1
    return [pl.BlockSpec((1, HALO, c), lambda bi, t: (bi, jnp.maximum(t * hb - 1, 0), 0)),
            pl.BlockSpec((1, tm, c), lambda bi, t: (bi, t, 0)),
            pl.BlockSpec((1, HALO, c), lambda bi, t: (bi, jnp.minimum((t + 1) * hb, last), 0))]


def _conf_conv(u, conv_w, conv_b, ln_g, ln_b):
    b, l, c = u.shape
    tm = min(256, l)
    nt = l // tm
    const = lambda bi, t: (0, 0)
    return pl.pallas_call(
        functools.partial(_conf_conv_kernel, tm=tm, nt=nt, rc=256),
        grid=(b, nt),
        in_specs=_halo_specs(tm, l, c) + [pl.BlockSpec(conv_w.shape, const)] + [pl.BlockSpec((1, c), const)] * 3,
        out_specs=pl.BlockSpec((1, tm, c), lambda bi, t: (bi, t, 0)),
        out_shape=jax.ShapeDtypeStruct((b, l, c), BF16),
        scratch_shapes=[pltpu.VMEM((tm + 2 * HALO, c), F32), pltpu.VMEM((7, tm + 2 * HALO - 8, c), F32)],
        compiler_params=_cparams("arbitrary", "arbitrary"),
        name="conformer_conv",
    )(u, u, u, conv_w, conv_b, ln_g, ln_b)


def _attn_kernel(q_ref, kt_ref, v_ref, o_ref, *, chunks):
    q = q_ref[0]
    tq = q.shape[0]
    halves = []
    for half in range(2):
        m = jnp.full((tq, 1), -jnp.inf, F32)
        acc = jnp.zeros((tq, LANES), F32)
        for c0, cs in chunks:
            s = _dot(q, kt_ref[0, 0, half, :, c0:c0 + cs])
            m_new = jnp.maximum(m, jnp.max(s, axis=1, keepdims=True))
            p = jnp.exp(s - m_new)
            acc = jnp.exp(m - m_new) * acc + _dot(p.astype(BF16), v_ref[0, 0, half, c0:c0 + cs, :])
            m = m_new
        den = (1 - half) * HEAD_DIM
        halves.append(acc / acc[:, den:den + 1])
    lane = lax.broadcasted_iota(jnp.int32, (tq, LANES), 1)
    o_ref[0] = jnp.where(lane < HEAD_DIM, halves[0], halves[1]).astype(BF16)


def _kv_layouts(k, v):
    b, lk, _ = k.shape
    kt = k.reshape(b, lk, N_KV_HEADS, HEAD_DIM).transpose(0, 2, 3, 1)
    z = jnp.zeros_like(kt)
    kt = jnp.stack([jnp.concatenate([kt, z], axis=2), jnp.concatenate([z, kt], axis=2)], axis=2)
    vh = v.reshape(b, lk, N_KV_HEADS, HEAD_DIM).transpose(0, 2, 1, 3)
    pad = jnp.zeros_like(vh).at[..., 0].set(1.0)
    vv = jnp.stack([jnp.concatenate([vh, pad], axis=3), jnp.concatenate([pad, vh], axis=3)], axis=2)
    return kt, vv


def _key_chunks(lk, size=1024):
    head = lk % size
    chunks = [(0, head)] if head else []
    return tuple(chunks + [(c, size) for c in range(head, lk, size)])


def _attention(q, k, v):
    b, l, _ = q.shape
    lk = k.shape[1]
    kt, vv = _kv_layouts(k, v)
    tq = min(1024, l)
    n_pairs = D_ATTN // LANES
    pairs_per_kv = n_pairs // N_KV_HEADS
    return pl.pallas_call(
        functools.partial(_attn_kernel, chunks=_key_chunks(lk)),
        grid=(b, n_pairs, l // tq),
        in_specs=[pl.BlockSpec((1, tq, LANES), lambda bi, j, t: (bi, t, j)),
                  pl.BlockSpec((1, 1, 2, LANES, lk), lambda bi, j, t: (bi, j // pairs_per_kv, 0, 0, 0)),
                  pl.BlockSpec((1, 1, 2, lk, LANES), lambda bi, j, t: (bi, j // pairs_per_kv, 0, 0, 0))],
        out_specs=pl.BlockSpec((1, tq, LANES), lambda bi, j, t: (bi, t, j)),
        out_shape=jax.ShapeDtypeStruct((b, l, D_ATTN), BF16),
        compiler_params=_cparams("arbitrary", "arbitrary", "arbitrary"),
        name="attention",
    )(q, kt, vv)


def _even_out_kernel(cv_ref, at_ref, wo_ref, x_ref, g1_ref, nfg_ref, sh2_ref, sc2_ref, wrh_ref, wrl_ref,
                     xo_ref, h2_ref, aff_ref):
    y = _dot(cv_ref[0], wo_ref[:D_CONF, :]) + _dot(at_ref[0], wo_ref[D_CONF:, :])
    _post(x_ref[0], y, g1_ref[0], nfg_ref[...], sh2_ref[0], sc2_ref[0], wrh_ref[...], wrl_ref[...],
          xo_ref, h2_ref, aff_ref)


def _post_specs(tm, d):
    row = lambda bi, t: (bi, 0, 0)
    const = lambda bi, t: (0, 0)
    in_specs = [pl.BlockSpec((1, tm, d), lambda bi, t: (bi, t, 0)), pl.BlockSpec((1, 1, d), row),
                pl.BlockSpec((1, d), const), pl.BlockSpec((1, 1, d), row), pl.BlockSpec((1, 1, d), row),
                pl.BlockSpec((d, LANES), const), pl.BlockSpec((d, 2 * LANES), const)]
    nk = d // LANES
    out_specs = [pl.BlockSpec((1, tm, d), lambda bi, t: (bi, t, 0)),
                 pl.BlockSpec((1, tm * nk, LANES), lambda bi, t: (bi, t, 0)),
                 pl.BlockSpec((1, N_EXPERTS, tm), lambda bi, t: (bi, 0, t))]
    return in_specs, out_specs


def _post_out_shapes(b, l, d):
    return [jax.ShapeDtypeStruct((b, l, d), F32), jax.ShapeDtypeStruct((b, l * (d // LANES), LANES), F32),
            jax.ShapeDtypeStruct((b, N_EXPERTS, l), F32)]


def _even_out(conv, attn, wo_bf, x, g1, nf_g, sh2, sc2, wr_hi, wr_hilo):
    b, l, d = x.shape
    tm = min(512, l)
    tok = lambda bi, t: (bi, t, 0)
    post_in, post_out = _post_specs(tm, d)
    return pl.pallas_call(
        _even_out_kernel,
        grid=(b, l // tm),
        in_specs=[pl.BlockSpec((1, tm, D_CONF), tok), pl.BlockSpec((1, tm, D_ATTN), tok),
                  pl.BlockSpec(wo_bf.shape, lambda bi, t: (0, 0))] + post_in,
        out_specs=post_out,
        out_shape=_post_out_shapes(b, l, d),
        compiler_params=_cparams("arbitrary", "arbitrary"),
        name="even_out_proj",
    )(conv, attn, wo_bf, x, g1, nf_g, sh2, sc2, wr_hi, wr_hilo)


def _odd_in_kernel(x_ref, sh_ref, sc_ref, g_ref, w_ref, z_ref):
    h = _modulated_norm(x_ref[0], g_ref[...], sh_ref[0], sc_ref[0])
    p = _dot(h.astype(BF16), w_ref[...])
    ds_ = D_SHORT
    z_ref[0, :, :ds_] = p[:, 2 * ds_:3 * ds_] * p[:, :ds_]
    z_ref[0, :, ds_:2 * ds_] = p[:, 3 * ds_:]
    z_ref[0, :, 2 * ds_:] = p[:, ds_:2 * ds_]


def _odd_in(x, sh, sc, g, w_bf):
    b, l, d = x.shape
    tm = min(512, l)
    row = lambda bi, t: (bi, 0, 0)
    tok = lambda bi, t: (bi, t, 0)
    const = lambda bi, t: (0, 0)
    nz = 2 * D_SHORT + D_POOL
    return pl.pallas_call(
        _odd_in_kernel,
        grid=(b, l // tm),
        in_specs=[pl.BlockSpec((1, tm, d), tok), pl.BlockSpec((1, 1, d), row), pl.BlockSpec((1, 1, d), row),
                  pl.BlockSpec((1, d), const), pl.BlockSpec(w_bf.shape, const)],
        out_specs=pl.BlockSpec((1, tm, nz), tok),
        out_shape=jax.ShapeDtypeStruct((b, l, nz), F32),
        compiler_params=_cparams("arbitrary", "arbitrary"),
        name="odd_in_proj",
    )(x, sh, sc, g, w_bf)


def _odd_mix_kernel(zp_ref, zc_ref, zn_ref, cw_ref, pw_ref, ps_ref, wo_ref,
                    x_ref, g1_ref, nfg_ref, sh2_ref, sc2_ref, wrh_ref, wrl_ref,
                    xo_ref, h2_ref, aff_ref, xs_ref, *, tm, nt, n_tok):
    _fill_halo(xs_ref, zp_ref, zc_ref, zn_ref, tm, nt)
    ds_ = D_SHORT
    width = cw_ref.shape[0]
    pad = width // 2
    conv = jnp.zeros((tm, ds_), F32)
    for k in range(width):
        conv = conv + xs_ref[pl.ds(HALO - pad + k, tm), 0:ds_] * cw_ref[k:k + 1, :]
    short = xs_ref[pl.ds(HALO, tm), 2 * ds_:3 * ds_] * conv
    pos = pl.program_id(1) * tm + lax.broadcasted_iota(jnp.int32, (tm, POOL_GROUP), 0)
    pooled = []
    for gi, w in enumerate(POOL_WINDOWS):
        c0 = ds_ + gi * POOL_GROUP
        tot = jnp.zeros((tm, POOL_GROUP), F32)
        for dlt in range(-(w // 2), w // 2):
            tot = tot + xs_ref[pl.ds(HALO + dlt, tm), c0:c0 + POOL_GROUP]
        cnt = (jnp.minimum(pos + w // 2, n_tok) - jnp.maximum(pos - w // 2, 0)).astype(F32)
        diff = tot / cnt - xs_ref[pl.ds(HALO, tm), c0:c0 + POOL_GROUP]
        pooled.append(_dot(diff.astype(BF16), pw_ref[gi]))
    pool = jnp.concatenate(pooled, axis=-1) * ps_ref[...]
    y = _dot(short.astype(BF16), wo_ref[:ds_, :]) + _dot(pool.astype(BF16), wo_ref[ds_:, :])
    _post(x_ref[0], y, g1_ref[0], nfg_ref[...], sh2_ref[0], sc2_ref[0], wrh_ref[...], wrl_ref[...],
          xo_ref, h2_ref, aff_ref)


def _odd_mix(z, conv_w, pool_w_bf, pool_scale, wo_bf, x, g1, nf_g, sh2, sc2, wr_hi, wr_hilo):
    b, l, d = x.shape
    tm = min(256, l)
    nt = l // tm
    nz = z.shape[-1]
    const = lambda bi, t: (0, 0)
    post_in, post_out = _post_specs(tm, d)
    return pl.pallas_call(
        functools.partial(_odd_mix_kernel, tm=tm, nt=nt, n_tok=l),
        grid=(b, nt),
        in_specs=_halo_specs(tm, l, nz) + [pl.BlockSpec(conv_w.shape, const),
                                            pl.BlockSpec(pool_w_bf.shape, lambda bi, t: (0, 0, 0)),
                                            pl.BlockSpec((1, D_POOL), const),
                                            pl.BlockSpec(wo_bf.shape, const)] + post_in,
        out_specs=post_out,
        out_shape=_post_out_shapes(b, l, d),
        scratch_shapes=[pltpu.VMEM((tm + 2 * HALO, nz), F32)],
        compiler_params=_cparams("arbitrary", "arbitrary"),
        name="odd_mixer",
    )(z, z, z, conv_w, pool_w_bf, pool_scale, wo_bf, x, g1, nf_g, sh2, sc2, wr_hi, wr_hilo)


def _route_kernel(aff_ref, tri_ref, pos_ref, *, cap):
    key = lax.bitcast_convert_type(aff_ref[0], jnp.int32)
    n = key.shape[1]
    capf = float(cap)
    thr = jnp.zeros((N_EXPERTS, 1), jnp.int32)
    for bit in range(30, -1, -1):
        cand = thr | (1 << bit)
        cnt = jnp.sum(jnp.where(key >= cand, 1.0, 0.0), axis=1, keepdims=True)
        thr = jnp.where(cnt >= capf, cand, thr)
    n_gt = jnp.sum(jnp.where(key > thr, 1.0, 0.0), axis=1, keepdims=True)
    need = capf - n_gt
    tri = tri_ref[...]
    off_eq = jnp.zeros((N_EXPERTS, 1), F32)
    off_sel = jnp.zeros((N_EXPERTS, 1), F32)
    for j in range(n // LANES):
        kj = key[:, j * LANES:(j + 1) * LANES]
        gt = kj > thr
        eqf = jnp.where(kj == thr, 1.0, 0.0)
        rank = _dot(eqf.astype(BF16), tri) + off_eq - eqf
        self_ = jnp.where(gt, 1.0, jnp.where(rank < need, eqf, 0.0))
        slot = _dot(self_.astype(BF16), tri) + off_sel - 1.0
        pos_ref[0, :, j * LANES:(j + 1) * LANES] = jnp.where(self_ > 0.0, slot, -1.0)
        off_eq = off_eq + jnp.sum(eqf, axis=1, keepdims=True)
        off_sel = off_sel + jnp.sum(self_, axis=1, keepdims=True)


def _route(aff_t, cap):
    b, e, n = aff_t.shape
    tri = jnp.asarray(np.triu(np.ones((LANES, LANES), np.float32)), BF16)
    return pl.pallas_call(
        functools.partial(_route_kernel, cap=cap),
        grid=(b,),
        in_specs=[pl.BlockSpec((1, e, n), lambda bi: (bi, 0, 0)), pl.BlockSpec((LANES, LANES), lambda bi: (0, 0))],
        out_specs=pl.BlockSpec((1, e, n), lambda bi: (bi, 0, 0)),
        out_shape=jax.ShapeDtypeStruct((b, e, n), F32),
        compiler_params=_cparams("arbitrary"),
        name="route",
    )(aff_t, tri)


ROWS_PER_STEP = 2048


def _experts_per_step(cap):
    return max(1, min(N_EXPERTS, ROWS_PER_STEP // cap))


def _slot_index_kernel(pos_ref, idx_ref, *, cap, sc, eps):
    n = pos_ref.shape[3]
    for ee in range(eps):
        pos = pos_ref[0, ee]
        for c in range(cap // sc):
            slot = (lax.broadcasted_iota(jnp.int32, (sc, LANES), 0) + c * sc).astype(F32)
            tok_acc = jnp.zeros((sc, LANES), F32)
            for j in range(n // LANES):
                tok = (lax.broadcasted_iota(jnp.int32, (1, LANES), 1) + j * LANES).astype(F32)
                tok_acc = tok_acc + jnp.where(slot == pos[:, j * LANES:(j + 1) * LANES], tok, 0.0)
            idx_ref[0, ee, :, c * sc:(c + 1) * sc] = jnp.sum(tok_acc.T, axis=0, keepdims=True).astype(jnp.int32)


def _slot_index(pos, cap):
    b, e, n = pos.shape
    sc = min(LANES, cap)
    eps = _experts_per_step(cap)
    sel = lambda bi, ei: (bi, ei, 0, 0)
    return pl.pallas_call(
        functools.partial(_slot_index_kernel, cap=cap, sc=sc, eps=eps),
        grid=(b, e // eps),
        in_specs=[pl.BlockSpec((1, eps, 1, n), sel)],
        out_specs=pl.BlockSpec((1, eps, 1, cap), sel),
        out_shape=jax.ShapeDtypeStruct((b, e, 1, cap), jnp.int32),
        compiler_params=_cparams("arbitrary", "arbitrary"),
        name="slot_index",
    )(pos.reshape(b, e, 1, n))


def _gather_kernel(idx_ref, h3_ref, xe_ref, tile_ref, *, cap, nk, stride, eps):
    for ee in range(eps):
        for s in range(cap):
            t = idx_ref[0, ee, 0, s]
            tile_ref[ee, pl.ds(s, nk, stride=stride), :] = h3_ref[0, pl.ds(pl.multiple_of(t * nk, nk), nk), :]
        for k in range(nk):
            xe_ref[ee, :, k * LANES:(k + 1) * LANES] = tile_ref[ee, k * stride:k * stride + cap, :].astype(BF16)


def _gather(idx, h3, cap):
    b, e = idx.shape[:2]
    rows = h3.shape[1]
    nk = 8
    d = nk * LANES
    stride = cap + 8
    eps = _experts_per_step(cap)
    return pl.pallas_call(
        functools.partial(_gather_kernel, cap=cap, nk=nk, stride=stride, eps=eps),
        grid=(b, e // eps),
        in_specs=[pl.BlockSpec((1, eps, 1, cap), lambda bi, ei: (bi, ei, 0, 0), memory_space=pltpu.SMEM),
                  pl.BlockSpec((1, rows, LANES), lambda bi, ei: (bi, 0, 0))],
        out_specs=pl.BlockSpec((eps, cap, d), lambda bi, ei: (ei, bi, 0)),
        out_shape=jax.ShapeDtypeStruct((e, b * cap, d), BF16),
        scratch_shapes=[pltpu.VMEM((eps, nk * stride, LANES), F32)],
        compiler_params=_cparams("arbitrary", "arbitrary"),
        name="moe_gather",
    )(idx, h3)


def _ffn_kernel(xe_ref, wg_ref, wu_ref, wd_ref, ye_ref, acc_ref, wg_bf, wu_bf, wd_bf, *, nf, tm):
    f = pl.program_id(1)
    m = pl.program_id(2)

    @pl.when(m == 0)
    def _():
        wg_bf[...] = wg_ref[0, 0].astype(BF16)
        wu_bf[...] = wu_ref[0, 0].astype(BF16)
        wd_bf[...] = wd_ref[0, 0].astype(BF16)

    rows = pl.ds(pl.multiple_of(m * tm, tm), tm)

    def step(first, last):
        x = xe_ref[0]
        g = _dot(x, wg_bf[...])
        u = _dot(x, wu_bf[...])
        tot = _dot((g * _sigmoid(g) * u).astype(BF16), wd_bf[...])
        if not first:
            tot = acc_ref[rows, :] + tot
        if last:
            _store_token_major(ye_ref, tot)
        else:
            acc_ref[rows, :] = tot

    pl.when(f == 0)(lambda: step(True, False))
    pl.when(jnp.logical_and(f > 0, f < nf - 1))(lambda: step(False, False))
    pl.when(f == nf - 1)(lambda: step(False, True))


def _expert_ffn(xe, w_gate, w_up, w_down, layer):
    e, m, d = xe.shape
    dff = w_gate.shape[-1]
    nk = d // LANES
    tm = min(1024, m)
    tf = 512
    nf = dff // tf
    out_tile = lambda ei, f, mi: (ei, jnp.where(f == nf - 1, mi, 0), 0)
    return pl.pallas_call(
        functools.partial(_ffn_kernel, nf=nf, tm=tm),
        grid=(e, nf, m // tm),
        in_specs=[pl.BlockSpec((1, tm, d), lambda ei, f, mi: (ei, mi, 0)),
                  pl.BlockSpec((1, 1, d, tf), lambda ei, f, mi: (layer, ei, 0, f)),
                  pl.BlockSpec((1, 1, d, tf), lambda ei, f, mi: (layer, ei, 0, f)),
                  pl.BlockSpec((1, 1, tf, d), lambda ei, f, mi: (layer, ei, f, 0))],
        out_specs=pl.BlockSpec((1, tm * nk, LANES), out_tile),
        out_shape=jax.ShapeDtypeStruct((e, m * nk, LANES), F32),
        scratch_shapes=[pltpu.VMEM((m, d), F32), pltpu.VMEM((d, tf), BF16), pltpu.VMEM((d, tf), BF16),
                        pltpu.VMEM((tf, d), BF16)],
        compiler_params=_cparams("arbitrary", "arbitrary", "arbitrary"),
        name="expert_ffn",
    )(xe, w_gate, w_up, w_down)


def _combine_kernel(idx_ref, aff_ref, ye_ref, x_ref, g2_ref, o_ref, acc_ref, *, cap, nk, tm, batch, eps, n_scatter):
    k = pl.program_id(1)

    @pl.when(k == 0)
    def _():
        acc_ref[...] = jnp.zeros_like(acc_ref)

    @pl.when(k < n_scatter)
    def _():
        for ee in range(eps):
            for s0 in range(0, cap, batch):
                rows = []
                for s in range(s0, s0 + batch):
                    t = idx_ref[0, ee, 0, s]
                    start = pl.multiple_of(t * nk, nk)
                    rows.append((start, acc_ref[pl.ds(start, nk), :]
                                 + aff_ref[0, ee, 0, t] * ye_ref[ee, s * nk:(s + 1) * nk, :]))
                for start, val in rows:
                    acc_ref[pl.ds(start, nk), :] = val

    @pl.when(k >= n_scatter)
    def _():
        view = acc_ref.at[pl.ds(pl.multiple_of((k - n_scatter) * (tm * nk), tm * nk), tm * nk), :]
        for c in range(nk):
            lanes = slice(c * LANES, (c + 1) * LANES)
            o_ref[0, :, lanes] = x_ref[0, :, lanes] + g2_ref[0, :, lanes] * view[pl.ds(c, tm, stride=nk), :]


def _combine(idx, aff_t, ye3, x, g2, cap):
    b, n, d = x.shape
    e = idx.shape[1]
    nk = d // LANES
    tm = min(1024, n)
    nt = n // tm
    eps = _experts_per_step(cap)
    ns = e // eps
    group = lambda bi, k: (bi, jnp.minimum(k, ns - 1), 0, 0)
    smem = lambda width: pl.BlockSpec((1, eps, 1, width), group, memory_space=pltpu.SMEM)
    tile = lambda bi, k: (bi, jnp.maximum(k - ns, 0), 0)
    return pl.pallas_call(
        functools.partial(_combine_kernel, cap=cap, nk=nk, tm=tm, batch=8, eps=eps, n_scatter=ns),
        grid=(b, ns + nt),
        in_specs=[smem(cap), smem(n),
                  pl.BlockSpec((eps, cap * nk, LANES), lambda bi, k: (jnp.minimum(k, ns - 1), bi, 0)),
                  pl.BlockSpec((1, tm, d), tile),
                  pl.BlockSpec((1, 1, d), lambda bi, k: (bi, 0, 0))],
        out_specs=pl.BlockSpec((1, tm, d), tile),
        out_shape=jax.ShapeDtypeStruct((b, n, d), F32),
        scratch_shapes=[pltpu.VMEM((n * nk, LANES), F32)],
        compiler_params=_cparams("arbitrary", "arbitrary"),
        name="moe_combine",
    )(idx, aff_t.reshape(b, e, 1, n), ye3, x, g2)


def _moe(x, h3, aff_t, g2, w_gate, w_up, w_down, layer):
    cap = CAPACITY_FACTOR * x.shape[1] // N_EXPERTS
    idx = _slot_index(_route(aff_t, cap), cap)
    ye3 = _expert_ffn(_gather(idx, h3, cap), w_gate, w_up, w_down, layer)
    return _combine(idx, aff_t, ye3, x, g2, cap)


def _rope_tables(n_tok):
    rows = n_tok // GRID_W
    row = np.repeat(np.arange(rows), GRID_W).astype(np.float32)
    col = np.tile(np.arange(GRID_W), rows).astype(np.float32)
    n_freq = HEAD_DIM // 4
    inv = jnp.asarray(ROPE_THETA, F32) ** (-jnp.arange(n_freq, dtype=F32) / n_freq)
    ang_r = jnp.asarray(row)[:, None] * inv
    ang_c = jnp.asarray(col)[:, None] * inv
    cr, sr, cc, sc = jnp.cos(ang_r), jnp.sin(ang_r), jnp.cos(ang_c), jnp.sin(ang_c)
    zero = jnp.zeros_like(sr)
    cos = jnp.concatenate([cr, cr, cc, cc], axis=-1)
    sin_a = jnp.concatenate([-sr, zero, -sc, zero], axis=-1)
    sin_b = jnp.concatenate([zero, sr, zero, sc], axis=-1)
    return tuple(jnp.tile(t, (1, N_Q_HEADS)) for t in (cos, sin_a, sin_b))


def _head_mean_matrix():
    blk = np.kron(np.eye(N_Q_HEADS, dtype=np.float32), np.full((HEAD_DIM, HEAD_DIM), 1.0 / HEAD_DIM, np.float32))
    return jnp.asarray(blk, BF16)


def _rows(m, b):
    return m[:b, None, :]


def kernel(x, c, ctx, c_ctx, norm_mix_g, norm_ffn_g, w_mod, b_mod, ev_w_in, ev_conv_w, ev_conv_b, ev_ln_g, ev_ln_b, ev_q_norm_g, ev_k_norm_g, ev_w_out, od_w_in, od_conv_w, od_pool_w, od_pool_scale, od_w_out, w_router, w_gate, w_up, w_down):
    b, l, d = x.shape
    depth = w_mod.shape[0]
    last_even = ((depth - 1) // 2) * 2

    n_rows = -(-(b + 1) // 8) * 8
    c_rows = jnp.zeros((n_rows, d), F32).at[:b].set(c).at[b].set(c_ctx)
    mods = _modulations(c_rows, w_mod, b_mod)
    rope = _rope_tables(l)
    bd = _head_mean_matrix()

    for i in range(depth):
        j = i // 2
        is_even = i % 2 == 0
        ctx_live = i < last_even
        m6 = mods[i].reshape(n_rows, 6, d)
        sh1, sc1, g1, sh2, sc2, g2 = [_rows(m6[:, t], b) for t in range(6)]
        mc = [jnp.broadcast_to(m6[b, t][None, None, :], (b, 1, d)) for t in range(6)]
        sh1c, sc1c, g1c, sh2c, sc2c, g2c = mc
        nm_g = norm_mix_g[i][None, :]
        nf_g = norm_ffn_g[i][None, :]
        wr_hi, wr_low = _split_bf16(jnp.pad(w_router[i], ((0, 0), (0, LANES - N_EXPERTS))))
        wr_hilo = jnp.concatenate([wr_hi, wr_low], axis=1)

        if is_even:
            w_in = ev_w_in[j].astype(BF16)
            wo = ev_w_out[j].astype(BF16)
            qg = jnp.tile(ev_q_norm_g[j], N_Q_HEADS)[None, :]
            kg = jnp.tile(ev_k_norm_g[j], N_KV_HEADS)[None, :]
            conv_args = (ev_conv_w[j], ev_conv_b[j][None, :], ev_ln_g[j][None, :], ev_ln_b[j][None, :])
            u, q, k, v = _even_in(x, sh1, sc1, nm_g, w_in, qg, kg, bd, rope)
            uc, qc, kc, vc = _even_in(ctx, sh1c, sc1c, nm_g, w_in, qg, kg, bd, None)
            attn = _attention(q, jnp.concatenate([kc, k], axis=1), jnp.concatenate([vc, v], axis=1))
            conv = _conf_conv(u, *conv_args)
            x, h3, aff = _even_out(conv, attn, wo, x, g1, nf_g, sh2, sc2, wr_hi, wr_hilo)
            if ctx_live:
                attn_c = _attention(qc, kc, vc)
                conv_c = _conf_conv(uc, *conv_args)
                ctx, h3c, affc = _even_out(conv_c, attn_c, wo, ctx, g1c, nf_g, sh2c, sc2c, wr_hi, wr_hilo)
        else:
            w_in = od_w_in[j].astype(BF16)
            wo = od_w_out[j].astype(BF16)
            mix_args = (od_conv_w[j], od_pool_w[j].astype(BF16), od_pool_scale[j][None, :], wo)
            z = _odd_in(x, sh1, sc1, nm_g, w_in)
            x, h3, aff = _odd_mix(z, *mix_args, x, g1, nf_g, sh2, sc2, wr_hi, wr_hilo)
            if ctx_live:
                zc = _odd_in(ctx, sh1c, sc1c, nm_g, w_in)
                ctx, h3c, affc = _odd_mix(zc, *mix_args, ctx, g1c, nf_g, sh2c, sc2c, wr_hi, wr_hilo)
        x = _moe(x, h3, aff, g2, w_gate, w_up, w_down, i)
        if ctx_live:
            ctx = _moe(ctx, h3c, affc, g2c, w_gate, w_up, w_down, i)
    return x
```

```python
import functools
import math

import jax
import jax.numpy as jnp
import numpy as np
from jax import lax
from jax.experimental import pallas as pl
from jax.experimental.pallas import tpu as pltpu

F32 = jnp.float32
BF16 = jnp.bfloat16

GRID_W = 64
N_Q_HEADS = 8
N_KV_HEADS = 2
HEAD_DIM = 64
D_ATTN = N_Q_HEADS * HEAD_DIM
D_KV = N_KV_HEADS * HEAD_DIM
ROPE_THETA = 10000.0
D_CONF = 512
D_SHORT = 512
D_POOL = 512
POOL_WINDOWS = (2, 4, 8, 16)
POOL_GROUP = D_POOL // len(POOL_WINDOWS)
N_EXPERTS = 16
CAPACITY_FACTOR = 2
EPS = 1e-6

LANES = 128
HALO = 16
VMEM_LIMIT = 56 * 1024 * 1024


def _sigmoid(x):
    return 1.0 / (1.0 + jnp.exp(-x))


def _dot(a, b):
    return jnp.dot(a, b, preferred_element_type=F32)


def _split_bf16(x):
    hi = x.astype(BF16)
    lo = (x - hi.astype(F32)).astype(BF16)
    return hi, lo


def _cparams(*sem):
    return pltpu.CompilerParams(dimension_semantics=sem, vmem_limit_bytes=VMEM_LIMIT)


def _mod_kernel(c_ref, w_ref, b_ref, o_ref):
    c = c_ref[...]
    s_hi, s_lo = _split_bf16(c * _sigmoid(c))
    w_hi, w_lo = _split_bf16(w_ref[0])
    o_ref[0] = _dot(s_hi, w_hi) + _dot(s_hi, w_lo) + _dot(s_lo, w_hi) + b_ref[0]


def _modulations(c_rows, w_mod, b_mod):
    depth, d, n = w_mod.shape
    r = c_rows.shape[0]
    tn = 1536
    return pl.pallas_call(
        _mod_kernel,
        grid=(depth, n // tn),
        in_specs=[pl.BlockSpec((r, d), lambda i, j: (0, 0)),
                  pl.BlockSpec((1, d, tn), lambda i, j: (i, 0, j)),
                  pl.BlockSpec((1, 1, tn), lambda i, j: (i, 0, j))],
        out_specs=pl.BlockSpec((1, r, tn), lambda i, j: (i, 0, j)),
        out_shape=jax.ShapeDtypeStruct((depth, r, n), F32),
        compiler_params=_cparams("arbitrary", "arbitrary"),
        name="modulations",
    )(c_rows, w_mod, b_mod.reshape(depth, 1, n))


def _modulated_norm(x, g, sh, sc):
    ms = jnp.mean(x * x, axis=-1, keepdims=True)
    return x * lax.rsqrt(ms + EPS) * g * (1.0 + sc) + sh


def _store_token_major(ref3, val):
    rows, d = val.shape
    nk = d // LANES
    for k in range(nk):
        ref3[0, pl.ds(k, rows, stride=nk), :] = val[:, k * LANES:(k + 1) * LANES]


def _post(x, y, g1, nf_g, sh2, sc2, wr_hi, wr_hilo, xo_ref, h3_ref, aff_ref):
    xn = x + g1 * y
    xo_ref[0] = xn
    h = _modulated_norm(xn, nf_g, sh2, sc2)
    h_hi, h_lo = _split_bf16(h)
    _store_token_major(h3_ref, h)
    both = _dot(h_hi, wr_hilo)
    lg = both[:, :LANES] + both[:, LANES:] + _dot(h_lo, wr_hi)
    lgt = lg.T[:N_EXPERTS]
    ex = jnp.exp(lgt - jnp.max(lgt, axis=0, keepdims=True))
    aff_ref[0] = ex / jnp.sum(ex, axis=0, keepdims=True)


def _head_rms(t, gain, bd):
    hi, lo = _split_bf16(t * t)
    ms = _dot(hi, bd) + _dot(lo, bd)
    return t * lax.rsqrt(ms + EPS) * gain


def _rope(t, cos, sin_a, sin_b):
    w = t.shape[-1]
    q = HEAD_DIM // 4
    return t * cos + pltpu.roll(t, w - q, 1) * sin_a + pltpu.roll(t, q, 1) * sin_b


def _even_in_kernel(*refs, rope):
    if rope:
        (x_ref, sh_ref, sc_ref, g_ref, w_ref, qg_ref, kg_ref, bd_ref, cos_ref, sa_ref, sb_ref,
         u_ref, q_ref, k_ref, v_ref) = refs
    else:
        x_ref, sh_ref, sc_ref, g_ref, w_ref, qg_ref, kg_ref, bd_ref, u_ref, q_ref, k_ref, v_ref = refs
    h = _modulated_norm(x_ref[0], g_ref[...], sh_ref[0], sc_ref[0])
    p = _dot(h.astype(BF16), w_ref[...])
    kv0 = 2 * D_CONF + D_ATTN
    u_ref[0] = p[:, :D_CONF] * _sigmoid(p[:, D_CONF:2 * D_CONF])
    q = _head_rms(p[:, 2 * D_CONF:kv0], qg_ref[...], bd_ref[...])
    k = _head_rms(p[:, kv0:kv0 + D_KV], kg_ref[...], bd_ref[:D_KV, :D_KV])
    if rope:
        cos, sa, sb = cos_ref[...], sa_ref[...], sb_ref[...]
        q = _rope(q, cos, sa, sb)
        k = _rope(k, cos[:, :D_KV], sa[:, :D_KV], sb[:, :D_KV])
    q_ref[0] = (q * (HEAD_DIM ** -0.5)).astype(BF16)
    k_ref[0] = k.astype(BF16)
    v_ref[0] = p[:, kv0 + D_KV:].astype(BF16)


def _even_in(x, sh, sc, g, w_bf, qg, kg, bd, rope_tabs):
    b, l, d = x.shape
    tm = min(512, l)
    n_in = w_bf.shape[1]
    row = lambda bi, t: (bi, 0, 0)
    tok = lambda bi, t: (bi, t, 0)
    const = lambda bi, t: (0, 0)
    in_specs = [pl.BlockSpec((1, tm, d), tok), pl.BlockSpec((1, 1, d), row), pl.BlockSpec((1, 1, d), row),
                pl.BlockSpec((1, d), const), pl.BlockSpec((d, n_in), const),
                pl.BlockSpec((1, D_ATTN), const), pl.BlockSpec((1, D_KV), const),
                pl.BlockSpec((D_ATTN, D_ATTN), const)]
    args = [x, sh, sc, g, w_bf, qg, kg, bd]
    if rope_tabs is not None:
        in_specs += [pl.BlockSpec((tm, D_ATTN), lambda bi, t: (t, 0))] * 3
        args += list(rope_tabs)
    return pl.pallas_call(
        functools.partial(_even_in_kernel, rope=rope_tabs is not None),
        grid=(b, l // tm),
        in_specs=in_specs,
        out_specs=[pl.BlockSpec((1, tm, D_CONF), tok), pl.BlockSpec((1, tm, D_ATTN), tok),
                   pl.BlockSpec((1, tm, D_KV), tok), pl.BlockSpec((1, tm, D_KV), tok)],
        out_shape=[jax.ShapeDtypeStruct((b, l, D_CONF), F32), jax.ShapeDtypeStruct((b, l, D_ATTN), BF16),
                   jax.ShapeDtypeStruct((b, l, D_KV), BF16), jax.ShapeDtypeStruct((b, l, D_KV), BF16)],
        compiler_params=_cparams("arbitrary", "arbitrary"),
        name="even_in_proj",
    )(*args)


def _fill_halo(xs_ref, prev_ref, cur_ref, next_ref, tm, nt):
    t = pl.program_id(1)
    xs_ref[0:HALO] = jnp.where(t > 0, prev_ref[0], 0.0)
    xs_ref[HALO:HALO + tm] = cur_ref[0]
    xs_ref[HALO + tm:HALO + tm + HALO] = jnp.where(t < nt - 1, next_ref[0], 0.0)


def _conf_conv_kernel(up_ref, uc_ref, un_ref, w_ref, b_ref, lg_ref, lb_ref, o_ref, xs_ref, sh_ref, *, tm, nt, rc):
    _fill_halo(xs_ref, up_ref, uc_ref, un_ref, tm, nt)
    width = w_ref.shape[0]
    pad = width // 2
    span = sh_ref.shape[1]
    for s in range(1, 8):
        sh_ref[s - 1] = xs_ref[pl.ds(s, span), :]
    for r in range(tm // rc):
        acc = jnp.zeros((rc, D_CONF), F32)
        for k in range(width):
            a, s = divmod(HALO - pad + k, 8)
            rows = pl.ds(r * rc + 8 * a, rc)
            win = xs_ref[rows, :] if s == 0 else sh_ref[s - 1, rows, :]
            acc = acc + win * w_ref[k:k + 1, :]
        u = acc + b_ref[...]
        mu = jnp.mean(u, axis=-1, keepdims=True)
        ctr = u - mu
        var = jnp.mean(ctr * ctr, axis=-1, keepdims=True)
        un = ctr * lax.rsqrt(var + EPS) * lg_ref[...] + lb_ref[...]
        o_ref[0, r * rc:(r + 1) * rc, :] = (un * _sigmoid(un)).astype(BF16)


def _halo_specs(tm, l, c):
    hb = tm // HALO
    last = l // HALO - 1
    return [pl.BlockSpec((1, HALO, c), lambda bi, t: (bi, jnp.maximum(t * hb - 1, 0), 0)),
            pl.BlockSpec((1, tm, c), lambda bi, t: (bi, t, 0)),
            pl.BlockSpec((1, HALO, c), lambda bi, t: (bi, jnp.minimum((t + 1) * hb, last), 0))]


def _conf_conv(u, conv_w, conv_b, ln_g, ln_b):
    b, l, c = u.shape
    tm = min(256, l)
    nt = l // tm
    const = lambda bi, t: (0, 0)
    return pl.pallas_call(
        functools.partial(_conf_conv_kernel, tm=tm, nt=nt, rc=256),
        grid=(b, nt),
        in_specs=_halo_specs(tm, l, c) + [pl.BlockSpec(conv_w.shape, const)] + [pl.BlockSpec((1, c), const)] * 3,
        out_specs=pl.BlockSpec((1, tm, c), lambda bi, t: (bi, t, 0)),
        out_shape=jax.ShapeDtypeStruct((b, l, c), BF16),
        scratch_shapes=[pltpu.VMEM((tm + 2 * HALO, c), F32), pltpu.VMEM((7, tm + 2 * HALO - 8, c), F32)],
        compiler_params=_cparams("arbitrary", "arbitrary"),
        name="conformer_conv",
    )(u, u, u, conv_w, conv_b, ln_g, ln_b)


def _attn_kernel(q_ref, kt_ref, v_ref, o_ref, *, chunks):
    q = q_ref[0]
    tq = q.shape[0]
    halves = []
    for half in range(2):
        m = jnp.full((tq, 1), -jnp.inf, F32)
        acc = jnp.zeros((tq, LANES), F32)
        for c0, cs in chunks:
            s = _dot(q, kt_ref[0, 0, half, :, c0:c0 + cs])
            m_new = jnp.maximum(m, jnp.max(s, axis=1, keepdims=True))
            p = jnp.exp(s - m_new)
            acc = jnp.exp(m - m_new) * acc + _dot(p.astype(BF16), v_ref[0, 0, half, c0:c0 + cs, :])
            m = m_new
        den = (1 - half) * HEAD_DIM
        halves.append(acc / acc[:, den:den + 1])
    lane = lax.broadcasted_iota(jnp.int32, (tq, LANES), 1)
    o_ref[0] = jnp.where(lane < HEAD_DIM, halves[0], halves[1]).astype(BF16)


def _kv_layouts(k, v):
    b, lk, _ = k.shape
    kt = k.reshape(b, lk, N_KV_HEADS, HEAD_DIM).transpose(0, 2, 3, 1)
    z = jnp.zeros_like(kt)
    kt = jnp.stack([jnp.concatenate([kt, z], axis=2), jnp.concatenate([z, kt], axis=2)], axis=2)
    vh = v.reshape(b, lk, N_KV_HEADS, HEAD_DIM).transpose(0, 2, 1, 3)
    pad = jnp.zeros_like(vh).at[..., 0].set(1.0)
    vv = jnp.stack([jnp.concatenate([vh, pad], axis=3), jnp.concatenate([pad, vh], axis=3)], axis=2)
    return kt, vv


def _key_chunks(lk, size=1024):
    head = lk % size
    chunks = [(0, head)] if head else []
    return tuple(chunks + [(c, size) for c in range(head, lk, size)])


def _attention(q, k, v):
    b, l, _ = q.shape
    lk = k.shape[1]
    kt, vv = _kv_layouts(k, v)
    tq = min(1024, l)
    n_pairs = D_ATTN // LANES
    pairs_per_kv = n_pairs // N_KV_HEADS
    return pl.pallas_call(
        functools.partial(_attn_kernel, chunks=_key_chunks(lk)),
        grid=(b, n_pairs, l // tq),
        in_specs=[pl.BlockSpec((1, tq, LANES), lambda bi, j, t: (bi, t, j)),
                  pl.BlockSpec((1, 1, 2, LANES, lk), lambda bi, j, t: (bi, j // pairs_per_kv, 0, 0, 0)),
                  pl.BlockSpec((1, 1, 2, lk, LANES), lambda bi, j, t: (bi, j // pairs_per_kv, 0, 0, 0))],
        out_specs=pl.BlockSpec((1, tq, LANES), lambda bi, j, t: (bi, t, j)),
        out_shape=jax.ShapeDtypeStruct((b, l, D_ATTN), BF16),
        compiler_params=_cparams("arbitrary", "arbitrary", "arbitrary"),
        name="attention",
    )(q, kt, vv)


def _even_out_kernel(cv_ref, at_ref, wo_ref, x_ref, g1_ref, nfg_ref, sh2_ref, sc2_ref, wrh_ref, wrl_ref,
                     xo_ref, h2_ref, aff_ref):
    y = _dot(cv_ref[0], wo_ref[:D_CONF, :]) + _dot(at_ref[0], wo_ref[D_CONF:, :])
    _post(x_ref[0], y, g1_ref[0], nfg_ref[...], sh2_ref[0], sc2_ref[0], wrh_ref[...], wrl_ref[...],
          xo_ref, h2_ref, aff_ref)


def _post_specs(tm, d):
    row = lambda bi, t: (bi, 0, 0)
    const = lambda bi, t: (0, 0)
    in_specs = [pl.BlockSpec((1, tm, d), lambda bi, t: (bi, t, 0)), pl.BlockSpec((1, 1, d), row),
                pl.BlockSpec((1, d), const), pl.BlockSpec((1, 1, d), row), pl.BlockSpec((1, 1, d), row),
                pl.BlockSpec((d, LANES), const), pl.BlockSpec((d, 2 * LANES), const)]
    nk = d // LANES
    out_specs = [pl.BlockSpec((1, tm, d), lambda bi, t: (bi, t, 0)),
                 pl.BlockSpec((1, tm * nk, LANES), lambda bi, t: (bi, t, 0)),
                 pl.BlockSpec((1, N_EXPERTS, tm), lambda bi, t: (bi, 0, t))]
    return in_specs, out_specs


def _post_out_shapes(b, l, d):
    return [jax.ShapeDtypeStruct((b, l, d), F32), jax.ShapeDtypeStruct((b, l * (d // LANES), LANES), F32),
            jax.ShapeDtypeStruct((b, N_EXPERTS, l), F32)]


def _even_out(conv, attn, wo_bf, x, g1, nf_g, sh2, sc2, wr_hi, wr_hilo):
    b, l, d = x.shape
    tm = min(512, l)
    tok = lambda bi, t: (bi, t, 0)
    post_in, post_out = _post_specs(tm, d)
    return pl.pallas_call(
        _even_out_kernel,
        grid=(b, l // tm),
        in_specs=[pl.BlockSpec((1, tm, D_CONF), tok), pl.BlockSpec((1, tm, D_ATTN), tok),
                  pl.BlockSpec(wo_bf.shape, lambda bi, t: (0, 0))] + post_in,
        out_specs=post_out,
        out_shape=_post_out_shapes(b, l, d),
        compiler_params=_cparams("arbitrary", "arbitrary"),
        name="even_out_proj",
    )(conv, attn, wo_bf, x, g1, nf_g, sh2, sc2, wr_hi, wr_hilo)


def _odd_in_kernel(x_ref, sh_ref, sc_ref, g_ref, w_ref, z_ref):
    h = _modulated_norm(x_ref[0], g_ref[...], sh_ref[0], sc_ref[0])
    p = _dot(h.astype(BF16), w_ref[...])
    ds_ = D_SHORT
    z_ref[0, :, :ds_] = p[:, 2 * ds_:3 * ds_] * p[:, :ds_]
    z_ref[0, :, ds_:2 * ds_] = p[:, 3 * ds_:]
    z_ref[0, :, 2 * ds_:] = p[:, ds_:2 * ds_]


def _odd_in(x, sh, sc, g, w_bf):
    b, l, d = x.shape
    tm = min(512, l)
    row = lambda bi, t: (bi, 0, 0)
    tok = lambda bi, t: (bi, t, 0)
    const = lambda bi, t: (0, 0)
    nz = 2 * D_SHORT + D_POOL
    return pl.pallas_call(
        _odd_in_kernel,
        grid=(b, l // tm),
        in_specs=[pl.BlockSpec((1, tm, d), tok), pl.BlockSpec((1, 1, d), row), pl.BlockSpec((1, 1, d), row),
                  pl.BlockSpec((1, d), const), pl.BlockSpec(w_bf.shape, const)],
        out_specs=pl.BlockSpec((1, tm, nz), tok),
        out_shape=jax.ShapeDtypeStruct((b, l, nz), F32),
        compiler_params=_cparams("arbitrary", "arbitrary"),
        name="odd_in_proj",
    )(x, sh, sc, g, w_bf)


def _odd_mix_kernel(zp_ref, zc_ref, zn_ref, cw_ref, pw_ref, ps_ref, wo_ref,
                    x_ref, g1_ref, nfg_ref, sh2_ref, sc2_ref, wrh_ref, wrl_ref,
                    xo_ref, h2_ref, aff_ref, xs_ref, *, tm, nt, n_tok):
    _fill_halo(xs_ref, zp_ref, zc_ref, zn_ref, tm, nt)
    ds_ = D_SHORT
    width = cw_ref.shape[0]
    pad = width // 2
    conv = jnp.zeros((tm, ds_), F32)
    for k in range(width):
        conv = conv + xs_ref[pl.ds(HALO - pad + k, tm), 0:ds_] * cw_ref[k:k + 1, :]
    short = xs_ref[pl.ds(HALO, tm), 2 * ds_:3 * ds_] * conv
    pos = pl.program_id(1) * tm + lax.broadcasted_iota(jnp.int32, (tm, POOL_GROUP), 0)
    pooled = []
    for gi, w in enumerate(POOL_WINDOWS):
        c0 = ds_ + gi * POOL_GROUP
        tot = jnp.zeros((tm, POOL_GROUP), F32)
        for dlt in range(-(w // 2), w // 2):
            tot = tot + xs_ref[pl.ds(HALO + dlt, tm), c0:c0 + POOL_GROUP]
        cnt = (jnp.minimum(pos + w // 2, n_tok) - jnp.maximum(pos - w // 2, 0)).astype(F32)
        diff = tot / cnt - xs_ref[pl.ds(HALO, tm), c0:c0 + POOL_GROUP]
        pooled.append(_dot(diff.astype(BF16), pw_ref[gi]))
    pool = jnp.concatenate(pooled, axis=-1) * ps_ref[...]
    y = _dot(short.astype(BF16), wo_ref[:ds_, :]) + _dot(pool.astype(BF16), wo_ref[ds_:, :])
    _post(x_ref[0], y, g1_ref[0], nfg_ref[...], sh2_ref[0], sc2_ref[0], wrh_ref[...], wrl_ref[...],
          xo_ref, h2_ref, aff_ref)


def _odd_mix(z, conv_w, pool_w_bf, pool_scale, wo_bf, x, g1, nf_g, sh2, sc2, wr_hi, wr_hilo):
    b, l, d = x.shape
    tm = min(256, l)
    nt = l // tm
    nz = z.shape[-1]
    const = lambda bi, t: (0, 0)
    post_in, post_out = _post_specs(tm, d)
    return pl.pallas_call(
        functools.partial(_odd_mix_kernel, tm=tm, nt=nt, n_tok=l),
        grid=(b, nt),
        in_specs=_halo_specs(tm, l, nz) + [pl.BlockSpec(conv_w.shape, const),
                                            pl.BlockSpec(pool_w_bf.shape, lambda bi, t: (0, 0, 0)),
                                            pl.BlockSpec((1, D_POOL), const),
                                            pl.BlockSpec(wo_bf.shape, const)] + post_in,
        out_specs=post_out,
        out_shape=_post_out_shapes(b, l, d),
        scratch_shapes=[pltpu.VMEM((tm + 2 * HALO, nz), F32)],
        compiler_params=_cparams("arbitrary", "arbitrary"),
        name="odd_mixer",
    )(z, z, z, conv_w, pool_w_bf, pool_scale, wo_bf, x, g1, nf_g, sh2, sc2, wr_hi, wr_hilo)


def _route_kernel(aff_ref, tri_ref, pos_ref, *, cap):
    key = lax.bitcast_convert_type(aff_ref[0], jnp.int32)
    n = key.shape[1]
    capf = float(cap)
    thr = jnp.zeros((N_EXPERTS, 1), jnp.int32)
    for bit in range(30, -1, -1):
        cand = thr | (1 << bit)
        cnt = jnp.sum(jnp.where(key >= cand, 1.0, 0.0), axis=1, keepdims=True)
        thr = jnp.where(cnt >= capf, cand, thr)
    n_gt = jnp.sum(jnp.where(key > thr, 1.0, 0.0), axis=1, keepdims=True)
    need = capf - n_gt
    tri = tri_ref[...]
    off_eq = jnp.zeros((N_EXPERTS, 1), F32)
    off_sel = jnp.zeros((N_EXPERTS, 1), F32)
    for j in range(n // LANES):
        kj = key[:, j * LANES:(j + 1) * LANES]
        gt = kj > thr
        eqf = jnp.where(kj == thr, 1.0, 0.0)
        rank = _dot(eqf.astype(BF16), tri) + off_eq - eqf
        self_ = jnp.where(gt, 1.0, jnp.where(rank < need, eqf, 0.0))
        slot = _dot(self_.astype(BF16), tri) + off_sel - 1.0
        pos_ref[0, :, j * LANES:(j + 1) * LANES] = jnp.where(self_ > 0.0, slot, -1.0)
        off_eq = off_eq + jnp.sum(eqf, axis=1, keepdims=True)
        off_sel = off_sel + jnp.sum(self_, axis=1, keepdims=True)


def _route(aff_t, cap):
    b, e, n = aff_t.shape
    tri = jnp.asarray(np.triu(np.ones((LANES, LANES), np.float32)), BF16)
    return pl.pallas_call(
        functools.partial(_route_kernel, cap=cap),
        grid=(b,),
        in_specs=[pl.BlockSpec((1, e, n), lambda bi: (bi, 0, 0)), pl.BlockSpec((LANES, LANES), lambda bi: (0, 0))],
        out_specs=pl.BlockSpec((1, e, n), lambda bi: (bi, 0, 0)),
        out_shape=jax.ShapeDtypeStruct((b, e, n), F32),
        compiler_params=_cparams("arbitrary"),
        name="route",
    )(aff_t, tri)


GATHER_ROWS_PER_STEP = 2048
COMBINE_ROWS_PER_STEP = 1024


def _experts_per_step(cap, rows_per_step=GATHER_ROWS_PER_STEP):
    return max(1, min(N_EXPERTS, rows_per_step // cap))


def _slot_index_kernel(pos_ref, idx_ref, *, cap, sc, eps):
    n = pos_ref.shape[3]
    for ee in range(eps):
        pos = pos_ref[0, ee]
        for c in range(cap // sc):
            slot = (lax.broadcasted_iota(jnp.int32, (sc, LANES), 0) + c * sc).astype(F32)
            tok_acc = jnp.zeros((sc, LANES), F32)
            for j in range(n // LANES):
                tok = (lax.broadcasted_iota(jnp.int32, (1, LANES), 1) + j * LANES).astype(F32)
                tok_acc = tok_acc + jnp.where(slot == pos[:, j * LANES:(j + 1) * LANES], tok, 0.0)
            idx_ref[0, ee, :, c * sc:(c + 1) * sc] = jnp.sum(tok_acc.T, axis=0, keepdims=True).astype(jnp.int32)


def _slot_index(pos, cap):
    b, e, n = pos.shape
    sc = min(LANES, cap)
    eps = _experts_per_step(cap)
    sel = lambda bi, ei: (bi, ei, 0, 0)
    return pl.pallas_call(
        functools.partial(_slot_index_kernel, cap=cap, sc=sc, eps=eps),
        grid=(b, e // eps),
        in_specs=[pl.BlockSpec((1, eps, 1, n), sel)],
        out_specs=pl.BlockSpec((1, eps, 1, cap), sel),
        out_shape=jax.ShapeDtypeStruct((b, e, 1, cap), jnp.int32),
        compiler_params=_cparams("arbitrary", "arbitrary"),
        name="slot_index",
    )(pos.reshape(b, e, 1, n))


def _gather_kernel(idx_ref, h3_ref, xe_ref, tile_ref, *, cap, nk, stride, eps):
    for ee in range(eps):
        for s in range(cap):
            t = idx_ref[0, ee, 0, s]
            tile_ref[ee, pl.ds(s, nk, stride=stride), :] = h3_ref[0, pl.ds(pl.multiple_of(t * nk, nk), nk), :]
        for k in range(nk):
            xe_ref[ee, :, k * LANES:(k + 1) * LANES] = tile_ref[ee, k * stride:k * stride + cap, :].astype(BF16)


def _gather(idx, h3, cap):
    b, e = idx.shape[:2]
    rows = h3.shape[1]
    nk = 8
    d = nk * LANES
    stride = cap + 8
    eps = _experts_per_step(cap)
    return pl.pallas_call(
        functools.partial(_gather_kernel, cap=cap, nk=nk, stride=stride, eps=eps),
        grid=(b, e // eps),
        in_specs=[pl.BlockSpec((1, eps, 1, cap), lambda bi, ei: (bi, ei, 0, 0), memory_space=pltpu.SMEM),
                  pl.BlockSpec((1, rows, LANES), lambda bi, ei: (bi, 0, 0))],
        out_specs=pl.BlockSpec((eps, cap, d), lambda bi, ei: (ei, bi, 0)),
        out_shape=jax.ShapeDtypeStruct((e, b * cap, d), BF16),
        scratch_shapes=[pltpu.VMEM((eps, nk * stride, LANES), F32)],
        compiler_params=_cparams("arbitrary", "arbitrary"),
        name="moe_gather",
    )(idx, h3)


def _ffn_kernel(*refs, nf, tm, nm, with_side):
    if with_side:
        (xe_ref, xs_ref, wg_ref, wu_ref, wd_ref, ye_ref, ys_ref, acc_ref, accs_ref, wg_bf, wu_bf, wd_bf) = refs
    else:
        xe_ref, wg_ref, wu_ref, wd_ref, ye_ref, acc_ref, wg_bf, wu_bf, wd_bf = refs
    f = pl.program_id(1)
    m = pl.program_id(2)

    @pl.when(m == 0)
    def _():
        wg_bf[...] = wg_ref[0, 0].astype(BF16)
        wu_bf[...] = wu_ref[0, 0].astype(BF16)
        wd_bf[...] = wd_ref[0, 0].astype(BF16)

    def step(x_ref, acc, out_ref, first, last):
        x = x_ref[0]
        g = _dot(x, wg_bf[...])
        u = _dot(x, wu_bf[...])
        tot = _dot((g * _sigmoid(g) * u).astype(BF16), wd_bf[...])
        if not first:
            tot = acc[...] + tot
        if last:
            _store_token_major(out_ref, tot)
        else:
            acc[...] = tot

    def three_ways(active, x_ref, acc, out_ref):
        pl.when(active & (f == 0))(lambda: step(x_ref, acc, out_ref, True, False))
        pl.when(active & (f > 0) & (f < nf - 1))(lambda: step(x_ref, acc, out_ref, False, False))
        pl.when(active & (f == nf - 1))(lambda: step(x_ref, acc, out_ref, False, True))

    row0 = pl.multiple_of(jnp.minimum(m, nm - 1) * tm, tm)
    three_ways(m < nm, xe_ref, acc_ref.at[pl.ds(row0, tm), :], ye_ref)
    if with_side:
        three_ways(m == nm, xs_ref, accs_ref, ys_ref)


def _expert_ffn(xe, w_gate, w_up, w_down, layer, xe_side=None):
    e, m, d = xe.shape
    dff = w_gate.shape[-1]
    nk = d // LANES
    tm = min(1024, m)
    nm = m // tm
    tf = 512
    nf = dff // tf
    with_side = xe_side is not None
    main_tile = lambda ei, f, mi: (ei, jnp.minimum(mi, nm - 1), 0)
    out_tile = lambda ei, f, mi: (ei, jnp.where(f == nf - 1, jnp.minimum(mi, nm - 1), 0), 0)
    whole = lambda ei, f, mi: (ei, 0, 0)
    w_specs = [pl.BlockSpec((1, 1, d, tf), lambda ei, f, mi: (layer, ei, 0, f)),
               pl.BlockSpec((1, 1, d, tf), lambda ei, f, mi: (layer, ei, 0, f)),
               pl.BlockSpec((1, 1, tf, d), lambda ei, f, mi: (layer, ei, f, 0))]
    in_specs = [pl.BlockSpec((1, tm, d), main_tile)]
    out_specs = [pl.BlockSpec((1, tm * nk, LANES), out_tile)]
    out_shape = [jax.ShapeDtypeStruct((e, m * nk, LANES), F32)]
    scratch = [pltpu.VMEM((m, d), F32)]
    args = [xe]
    if with_side:
        ms = xe_side.shape[1]
        in_specs.append(pl.BlockSpec((1, ms, d), whole))
        out_specs.append(pl.BlockSpec((1, ms * nk, LANES), whole))
        out_shape.append(jax.ShapeDtypeStruct((e, ms * nk, LANES), F32))
        scratch.append(pltpu.VMEM((ms, d), F32))
        args.append(xe_side)
    scratch += [pltpu.VMEM((d, tf), BF16), pltpu.VMEM((d, tf), BF16), pltpu.VMEM((tf, d), BF16)]
    outs = pl.pallas_call(
        functools.partial(_ffn_kernel, nf=nf, tm=tm, nm=nm, with_side=with_side),
        grid=(e, nf, nm + int(with_side)),
        in_specs=in_specs + w_specs,
        out_specs=out_specs,
        out_shape=out_shape,
        scratch_shapes=scratch,
        compiler_params=_cparams("arbitrary", "arbitrary", "arbitrary"),
        name="expert_ffn",
    )(*args, w_gate, w_up, w_down)
    return outs


def _combine_kernel(idx_ref, aff_ref, ye_ref, x_ref, g2_ref, o_ref, acc_ref, *, cap, nk, tm, batch, eps, n_scatter):
    k = pl.program_id(1)

    @pl.when(k == 0)
    def _():
        acc_ref[...] = jnp.zeros_like(acc_ref)

    @pl.when(k < n_scatter)
    def _():
        for ee in range(eps):
            for s0 in range(0, cap, batch):
                rows = []
                for s in range(s0, s0 + batch):
                    t = idx_ref[0, ee, 0, s]
                    start = pl.multiple_of(t * nk, nk)
                    rows.append((start, acc_ref[pl.ds(start, nk), :]
                                 + aff_ref[0, ee, 0, t] * ye_ref[ee, s * nk:(s + 1) * nk, :]))
                for start, val in rows:
                    acc_ref[pl.ds(start, nk), :] = val

    @pl.when(k >= n_scatter)
    def _():
        view = acc_ref.at[pl.ds(pl.multiple_of((k - n_scatter) * (tm * nk), tm * nk), tm * nk), :]
        for c in range(nk):
            lanes = slice(c * LANES, (c + 1) * LANES)
            o_ref[0, :, lanes] = x_ref[0, :, lanes] + g2_ref[0, :, lanes] * view[pl.ds(c, tm, stride=nk), :]


def _combine(idx, aff_t, ye3, x, g2, cap):
    b, n, d = x.shape
    e = idx.shape[1]
    nk = d // LANES
    tm = min(1024, n)
    nt = n // tm
    eps = _experts_per_step(cap, COMBINE_ROWS_PER_STEP)
    ns = e // eps
    group = lambda bi, k: (bi, jnp.minimum(k, ns - 1), 0, 0)
    smem = lambda width: pl.BlockSpec((1, eps, 1, width), group, memory_space=pltpu.SMEM)
    tile = lambda bi, k: (bi, jnp.maximum(k - ns, 0), 0)
    return pl.pallas_call(
        functools.partial(_combine_kernel, cap=cap, nk=nk, tm=tm, batch=8, eps=eps, n_scatter=ns),
        grid=(b, ns + nt),
        in_specs=[smem(cap), smem(n),
                  pl.BlockSpec((eps, cap * nk, LANES), lambda bi, k: (jnp.minimum(k, ns - 1), bi, 0)),
                  pl.BlockSpec((1, tm, d), tile),
                  pl.BlockSpec((1, 1, d), lambda bi, k: (bi, 0, 0))],
        out_specs=pl.BlockSpec((1, tm, d), tile),
        out_shape=jax.ShapeDtypeStruct((b, n, d), F32),
        scratch_shapes=[pltpu.VMEM((n * nk, LANES), F32)],
        compiler_params=_cparams("arbitrary", "arbitrary"),
        name="moe_combine",
    )(idx, aff_t.reshape(b, e, 1, n), ye3, x, g2)


def _moe(streams, w_gate, w_up, w_down, layer):
    caps = [CAPACITY_FACTOR * x.shape[1] // N_EXPERTS for x, _, _, _ in streams]
    idxs = [_slot_index(_route(aff_t, cap), cap) for (_, _, aff_t, _), cap in zip(streams, caps)]
    xes = [_gather(idx, h3, cap) for (_, h3, _, _), idx, cap in zip(streams, idxs, caps)]
    ye3s = _expert_ffn(xes[0], w_gate, w_up, w_down, layer, *xes[1:])
    return [_combine(idx, aff_t, ye3, x, g2, cap)
            for (x, _, aff_t, g2), idx, ye3, cap in zip(streams, idxs, ye3s, caps)]


def _rope_tables(n_tok):
    rows = n_tok // GRID_W
    row = np.repeat(np.arange(rows), GRID_W).astype(np.float32)
    col = np.tile(np.arange(GRID_W), rows).astype(np.float32)
    n_freq = HEAD_DIM // 4
    inv = jnp.asarray(ROPE_THETA, F32) ** (-jnp.arange(n_freq, dtype=F32) / n_freq)
    ang_r = jnp.asarray(row)[:, None] * inv
    ang_c = jnp.asarray(col)[:, None] * inv
    cr, sr, cc, sc = jnp.cos(ang_r), jnp.sin(ang_r), jnp.cos(ang_c), jnp.sin(ang_c)
    zero = jnp.zeros_like(sr)
    cos = jnp.concatenate([cr, cr, cc, cc], axis=-1)
    sin_a = jnp.concatenate([-sr, zero, -sc, zero], axis=-1)
    sin_b = jnp.concatenate([zero, sr, zero, sc], axis=-1)
    return tuple(jnp.tile(t, (1, N_Q_HEADS)) for t in (cos, sin_a, sin_b))


def _head_mean_matrix():
    blk = np.kron(np.eye(N_Q_HEADS, dtype=np.float32), np.full((HEAD_DIM, HEAD_DIM), 1.0 / HEAD_DIM, np.float32))
    return jnp.asarray(blk, BF16)


def _rows(m, b):
    return m[:b, None, :]


def kernel(x, c, ctx, c_ctx, norm_mix_g, norm_ffn_g, w_mod, b_mod, ev_w_in, ev_conv_w, ev_conv_b, ev_ln_g, ev_ln_b, ev_q_norm_g, ev_k_norm_g, ev_w_out, od_w_in, od_conv_w, od_pool_w, od_pool_scale, od_w_out, w_router, w_gate, w_up, w_down):
    b, l, d = x.shape
    depth = w_mod.shape[0]
    last_even = ((depth - 1) // 2) * 2

    n_rows = -(-(b + 1) // 8) * 8
    c_rows = jnp.zeros((n_rows, d), F32).at[:b].set(c).at[b].set(c_ctx)
    mods = _modulations(c_rows, w_mod, b_mod)
    rope = _rope_tables(l)
    bd = _head_mean_matrix()

    for i in range(depth):
        j = i // 2
        is_even = i % 2 == 0
        ctx_live = i < last_even
        m6 = mods[i].reshape(n_rows, 6, d)
        sh1, sc1, g1, sh2, sc2, g2 = [_rows(m6[:, t], b) for t in range(6)]
        mc = [jnp.broadcast_to(m6[b, t][None, None, :], (b, 1, d)) for t in range(6)]
        sh1c, sc1c, g1c, sh2c, sc2c, g2c = mc
        nm_g = norm_mix_g[i][None, :]
        nf_g = norm_ffn_g[i][None, :]
        wr_hi, wr_low = _split_bf16(jnp.pad(w_router[i], ((0, 0), (0, LANES - N_EXPERTS))))
        wr_hilo = jnp.concatenate([wr_hi, wr_low], axis=1)

        if is_even:
            w_in = ev_w_in[j].astype(BF16)
            wo = ev_w_out[j].astype(BF16)
            qg = jnp.tile(ev_q_norm_g[j], N_Q_HEADS)[None, :]
            kg = jnp.tile(ev_k_norm_g[j], N_KV_HEADS)[None, :]
            conv_args = (ev_conv_w[j], ev_conv_b[j][None, :], ev_ln_g[j][None, :], ev_ln_b[j][None, :])
            u, q, k, v = _even_in(x, sh1, sc1, nm_g, w_in, qg, kg, bd, rope)
            uc, qc, kc, vc = _even_in(ctx, sh1c, sc1c, nm_g, w_in, qg, kg, bd, None)
            attn = _attention(q, jnp.concatenate([kc, k], axis=1), jnp.concatenate([vc, v], axis=1))
            conv = _conf_conv(u, *conv_args)
            x, h3, aff = _even_out(conv, attn, wo, x, g1, nf_g, sh2, sc2, wr_hi, wr_hilo)
            if ctx_live:
                attn_c = _attention(qc, kc, vc)
                conv_c = _conf_conv(uc, *conv_args)
                ctx, h3c, affc = _even_out(conv_c, attn_c, wo, ctx, g1c, nf_g, sh2c, sc2c, wr_hi, wr_hilo)
        else:
            w_in = od_w_in[j].astype(BF16)
            wo = od_w_out[j].astype(BF16)
            mix_args = (od_conv_w[j], od_pool_w[j].astype(BF16), od_pool_scale[j][None, :], wo)
            z = _odd_in(x, sh1, sc1, nm_g, w_in)
            x, h3, aff = _odd_mix(z, *mix_args, x, g1, nf_g, sh2, sc2, wr_hi, wr_hilo)
            if ctx_live:
                zc = _odd_in(ctx, sh1c, sc1c, nm_g, w_in)
                ctx, h3c, affc = _odd_mix(zc, *mix_args, ctx, g1c, nf_g, sh2c, sc2c, wr_hi, wr_hilo)
        streams = [(x, h3, aff, g2)] + ([(ctx, h3c, affc, g2c)] if ctx_live else [])
        outs = _moe(streams, w_gate, w_up, w_down, i)
        x = outs[0]
        if ctx_live:
            ctx = outs[1]
    return x
```

```python
import functools
import math

import jax
import jax.numpy as jnp
import numpy as np
from jax import lax
from jax.experimental import pallas as pl
from jax.experimental.pallas import tpu as pltpu

F32 = jnp.float32
BF16 = jnp.bfloat16

GRID_W = 64
N_Q_HEADS = 8
N_KV_HEADS = 2
HEAD_DIM = 64
D_ATTN = N_Q_HEADS * HEAD_DIM
D_KV = N_KV_HEADS * HEAD_DIM
ROPE_THETA = 10000.0
D_CONF = 512
D_SHORT = 512
D_POOL = 512
POOL_WINDOWS = (2, 4, 8, 16)
POOL_GROUP = D_POOL // len(POOL_WINDOWS)
N_EXPERTS = 16
CAPACITY_FACTOR = 2
EPS = 1e-6

LANES = 128
HALO = 16
VMEM_LIMIT = 56 * 1024 * 1024


def _sigmoid(x):
    return 1.0 / (1.0 + jnp.exp(-x))


def _dot(a, b):
    return jnp.dot(a, b, preferred_element_type=F32)


def _split_bf16(x):
    hi = x.astype(BF16)
    lo = (x - hi.astype(F32)).astype(BF16)
    return hi, lo


def _cparams(*sem):
    return pltpu.CompilerParams(dimension_semantics=sem, vmem_limit_bytes=VMEM_LIMIT)


def _mod_kernel(c_ref, w_ref, b_ref, o_ref):
    c = c_ref[...]
    s_hi, s_lo = _split_bf16(c * _sigmoid(c))
    w_hi, w_lo = _split_bf16(w_ref[0])
    o_ref[0] = _dot(s_hi, w_hi) + _dot(s_hi, w_lo) + _dot(s_lo, w_hi) + b_ref[0]


def _modulations(c_rows, w_mod, b_mod):
    depth, d, n = w_mod.shape
    r = c_rows.shape[0]
    tn = 1536
    return pl.pallas_call(
        _mod_kernel,
        grid=(depth, n // tn),
        in_specs=[pl.BlockSpec((r, d), lambda i, j: (0, 0)),
                  pl.BlockSpec((1, d, tn), lambda i, j: (i, 0, j)),
                  pl.BlockSpec((1, 1, tn), lambda i, j: (i, 0, j))],
        out_specs=pl.BlockSpec((1, r, tn), lambda i, j: (i, 0, j)),
        out_shape=jax.ShapeDtypeStruct((depth, r, n), F32),
        compiler_params=_cparams("arbitrary", "arbitrary"),
        name="modulations",
    )(c_rows, w_mod, b_mod.reshape(depth, 1, n))


def _modulated_norm(x, g, sh, sc):
    ms = jnp.mean(x * x, axis=-1, keepdims=True)
    return x * lax.rsqrt(ms + EPS) * g * (1.0 + sc) + sh


def _store_token_major(ref3, val):
    rows, d = val.shape
    nk = d // LANES
    for k in range(nk):
        ref3[0, pl.ds(k, rows, stride=nk), :] = val[:, k * LANES:(k + 1) * LANES]


def _post(x, y, g1, nf_g, sh2, sc2, wr_hi, wr_hilo, xo_ref, h3_ref, aff_ref):
    xn = x + g1 * y
    xo_ref[0] = xn
    h = _modulated_norm(xn, nf_g, sh2, sc2)
    h_hi, h_lo = _split_bf16(h)
    _store_token_major(h3_ref, h)
    both = _dot(h_hi, wr_hilo)
    lg = both[:, :LANES] + both[:, LANES:] + _dot(h_lo, wr_hi)
    lgt = lg.T[:N_EXPERTS]
    ex = jnp.exp(lgt - jnp.max(lgt, axis=0, keepdims=True))
    aff_ref[0] = ex / jnp.sum(ex, axis=0, keepdims=True)


def _head_rms(t, gain, bd):
    hi, lo = _split_bf16(t * t)
    ms = _dot(hi, bd) + _dot(lo, bd)
    return t * lax.rsqrt(ms + EPS) * gain


def _rope(t, cos, sin_a, sin_b):
    w = t.shape[-1]
    q = HEAD_DIM // 4
    return t * cos + pltpu.roll(t, w - q, 1) * sin_a + pltpu.roll(t, q, 1) * sin_b


def _even_in_kernel(*refs, rope):
    if rope:
        (x_ref, sh_ref, sc_ref, g_ref, w_ref, qg_ref, kg_ref, bd_ref, cos_ref, sa_ref, sb_ref,
         u_ref, q_ref, k_ref, v_ref) = refs
    else:
        x_ref, sh_ref, sc_ref, g_ref, w_ref, qg_ref, kg_ref, bd_ref, u_ref, q_ref, k_ref, v_ref = refs
    h = _modulated_norm(x_ref[0], g_ref[...], sh_ref[0], sc_ref[0])
    p = _dot(h.astype(BF16), w_ref[...])
    kv0 = 2 * D_CONF + D_ATTN
    u_ref[0] = p[:, :D_CONF] * _sigmoid(p[:, D_CONF:2 * D_CONF])
    q = _head_rms(p[:, 2 * D_CONF:kv0], qg_ref[...], bd_ref[...])
    k = _head_rms(p[:, kv0:kv0 + D_KV], kg_ref[...], bd_ref[:D_KV, :D_KV])
    if rope:
        cos, sa, sb = cos_ref[...], sa_ref[...], sb_ref[...]
        q = _rope(q, cos, sa, sb)
        k = _rope(k, cos[:, :D_KV], sa[:, :D_KV], sb[:, :D_KV])
    q_ref[0] = (q * (HEAD_DIM ** -0.5)).astype(BF16)
    k_ref[0] = k.astype(BF16)
    v_ref[0] = p[:, kv0 + D_KV:].astype(BF16)


def _even_in(x, sh, sc, g, w_bf, qg, kg, bd, rope_tabs):
    b, l, d = x.shape
    tm = min(512, l)
    n_in = w_bf.shape[1]
    row = lambda bi, t: (bi, 0, 0)
    tok = lambda bi, t: (bi, t, 0)
    const = lambda bi, t: (0, 0)
    in_specs = [pl.BlockSpec((1, tm, d), tok), pl.BlockSpec((1, 1, d), row), pl.BlockSpec((1, 1, d), row),
                pl.BlockSpec((1, d), const), pl.BlockSpec((d, n_in), const),
                pl.BlockSpec((1, D_ATTN), const), pl.BlockSpec((1, D_KV), const),
                pl.BlockSpec((D_ATTN, D_ATTN), const)]
    args = [x, sh, sc, g, w_bf, qg, kg, bd]
    if rope_tabs is not None:
        in_specs += [pl.BlockSpec((tm, D_ATTN), lambda bi, t: (t, 0))] * 3
        args += list(rope_tabs)
    return pl.pallas_call(
        functools.partial(_even_in_kernel, rope=rope_tabs is not None),
        grid=(b, l // tm),
        in_specs=in_specs,
        out_specs=[pl.BlockSpec((1, tm, D_CONF), tok), pl.BlockSpec((1, tm, D_ATTN), tok),
                   pl.BlockSpec((1, tm, D_KV), tok), pl.BlockSpec((1, tm, D_KV), tok)],
        out_shape=[jax.ShapeDtypeStruct((b, l, D_CONF), F32), jax.ShapeDtypeStruct((b, l, D_ATTN), BF16),
                   jax.ShapeDtypeStruct((b, l, D_KV), BF16), jax.ShapeDtypeStruct((b, l, D_KV), BF16)],
        compiler_params=_cparams("arbitrary", "arbitrary"),
        name="even_in_proj",
    )(*args)


def _fill_halo(xs_ref, prev_ref, cur_ref, next_ref, tm, nt):
    t = pl.program_id(1)
    xs_ref[0:HALO] = jnp.where(t > 0, prev_ref[0], 0.0)
    xs_ref[HALO:HALO + tm] = cur_ref[0]
    xs_ref[HALO + tm:HALO + tm + HALO] = jnp.where(t < nt - 1, next_ref[0], 0.0)


def _conf_conv_kernel(up_ref, uc_ref, un_ref, w_ref, b_ref, lg_ref, lb_ref, o_ref, xs_ref, sh_ref, *, tm, nt, rc):
    _fill_halo(xs_ref, up_ref, uc_ref, un_ref, tm, nt)
    width = w_ref.shape[0]
    pad = width // 2
    span = sh_ref.shape[1]
    for s in range(1, 8):
        sh_ref[s - 1] = xs_ref[pl.ds(s, span), :]
    for r in range(tm // rc):
        acc = jnp.zeros((rc, D_CONF), F32)
        for k in range(width):
            a, s = divmod(HALO - pad + k, 8)
            rows = pl.ds(r * rc + 8 * a, rc)
            win = xs_ref[rows, :] if s == 0 else sh_ref[s - 1, rows, :]
            acc = acc + win * w_ref[k:k + 1, :]
        u = acc + b_ref[...]
        mu = jnp.mean(u, axis=-1, keepdims=True)
        ctr = u - mu
        var = jnp.mean(ctr * ctr, axis=-1, keepdims=True)
        un = ctr * lax.rsqrt(var + EPS) * lg_ref[...] + lb_ref[...]
        o_ref[0, r * rc:(r + 1) * rc, :] = (un * _sigmoid(un)).astype(BF16)


def _halo_specs(tm, l, c):
    hb = tm // HALO
    last = l // HALO - 1
    return [pl.BlockSpec((1, HALO, c), lambda bi, t: (bi, jnp.maximum(t * hb - 1, 0), 0)),
            pl.BlockSpec((1, tm, c), lambda bi, t: (bi, t, 0)),
            pl.BlockSpec((1, HALO, c), lambda bi, t: (bi, jnp.minimum((t + 1) * hb, last), 0))]


def _conf_conv(u, conv_w, conv_b, ln_g, ln_b):
    b, l, c = u.shape
    tm = min(256, l)
    nt = l // tm
    const = lambda bi, t: (0, 0)
    return pl.pallas_call(
        functools.partial(_conf_conv_kernel, tm=tm, nt=nt, rc=256),
        grid=(b, nt),
        in_specs=_halo_specs(tm, l, c) + [pl.BlockSpec(conv_w.shape, const)] + [pl.BlockSpec((1, c), const)] * 3,
        out_specs=pl.BlockSpec((1, tm, c), lambda bi, t: (bi, t, 0)),
        out_shape=jax.ShapeDtypeStruct((b, l, c), BF16),
        scratch_shapes=[pltpu.VMEM((tm + 2 * HALO, c), F32), pltpu.VMEM((7, tm + 2 * HALO - 8, c), F32)],
        compiler_params=_cparams("arbitrary", "arbitrary"),
        name="conformer_conv",
    )(u, u, u, conv_w, conv_b, ln_g, ln_b)


def _attn_kernel(q_ref, kt_ref, v_ref, o_ref, *, chunks):
    q = q_ref[0]
    tq = q.shape[0]
    halves = []
    for half in range(2):
        m = jnp.full((tq, 1), -jnp.inf, F32)
        acc = jnp.zeros((tq, LANES), F32)
        for c0, cs in chunks:
            s = _dot(q, kt_ref[0, 0, half, :, c0:c0 + cs])
            m_new = jnp.maximum(m, jnp.max(s, axis=1, keepdims=True))
            p = jnp.exp(s - m_new)
            acc = jnp.exp(m - m_new) * acc + _dot(p.astype(BF16), v_ref[0, 0, half, c0:c0 + cs, :])
            m = m_new
        den = (1 - half) * HEAD_DIM
        halves.append(acc / acc[:, den:den + 1])
    lane = lax.broadcasted_iota(jnp.int32, (tq, LANES), 1)
    o_ref[0] = jnp.where(lane < HEAD_DIM, halves[0], halves[1]).astype(BF16)


def _kv_layouts(k, v):
    b, lk, _ = k.shape
    kt = k.reshape(b, lk, N_KV_HEADS, HEAD_DIM).transpose(0, 2, 3, 1)
    z = jnp.zeros_like(kt)
    kt = jnp.stack([jnp.concatenate([kt, z], axis=2), jnp.concatenate([z, kt], axis=2)], axis=2)
    vh = v.reshape(b, lk, N_KV_HEADS, HEAD_DIM).transpose(0, 2, 1, 3)
    pad = jnp.zeros_like(vh).at[..., 0].set(1.0)
    vv = jnp.stack([jnp.concatenate([vh, pad], axis=3), jnp.concatenate([pad, vh], axis=3)], axis=2)
    return kt, vv


def _key_chunks(lk, size=1024):
    head = lk % size
    chunks = [(0, head)] if head else []
    return tuple(chunks + [(c, size) for c in range(head, lk, size)])


def _attention(q, k, v):
    b, l, _ = q.shape
    lk = k.shape[1]
    kt, vv = _kv_layouts(k, v)
    tq = min(1024, l)
    n_pairs = D_ATTN // LANES
    pairs_per_kv = n_pairs // N_KV_HEADS
    return pl.pallas_call(
        functools.partial(_attn_kernel, chunks=_key_chunks(lk)),
        grid=(b, n_pairs, l // tq),
        in_specs=[pl.BlockSpec((1, tq, LANES), lambda bi, j, t: (bi, t, j)),
                  pl.BlockSpec((1, 1, 2, LANES, lk), lambda bi, j, t: (bi, j // pairs_per_kv, 0, 0, 0)),
                  pl.BlockSpec((1, 1, 2, lk, LANES), lambda bi, j, t: (bi, j // pairs_per_kv, 0, 0, 0))],
        out_specs=pl.BlockSpec((1, tq, LANES), lambda bi, j, t: (bi, t, j)),
        out_shape=jax.ShapeDtypeStruct((b, l, D_ATTN), BF16),
        compiler_params=_cparams("arbitrary", "arbitrary", "arbitrary"),
        name="attention",
    )(q, kt, vv)


def _even_out_kernel(cv_ref, at_ref, wo_ref, x_ref, g1_ref, nfg_ref, sh2_ref, sc2_ref, wrh_ref, wrl_ref,
                     xo_ref, h2_ref, aff_ref):
    y = _dot(cv_ref[0], wo_ref[:D_CONF, :]) + _dot(at_ref[0], wo_ref[D_CONF:, :])
    _post(x_ref[0], y, g1_ref[0], nfg_ref[...], sh2_ref[0], sc2_ref[0], wrh_ref[...], wrl_ref[...],
          xo_ref, h2_ref, aff_ref)


def _post_specs(tm, d):
    row = lambda bi, t: (bi, 0, 0)
    const = lambda bi, t: (0, 0)
    in_specs = [pl.BlockSpec((1, tm, d), lambda bi, t: (bi, t, 0)), pl.BlockSpec((1, 1, d), row),
                pl.BlockSpec((1, d), const), pl.BlockSpec((1, 1, d), row), pl.BlockSpec((1, 1, d), row),
                pl.BlockSpec((d, LANES), const), pl.BlockSpec((d, 2 * LANES), const)]
    nk = d // LANES
    out_specs = [pl.BlockSpec((1, tm, d), lambda bi, t: (bi, t, 0)),
                 pl.BlockSpec((1, tm * nk, LANES), lambda bi, t: (bi, t, 0)),
                 pl.BlockSpec((1, N_EXPERTS, tm), lambda bi, t: (bi, 0, t))]
    return in_specs, out_specs


def _post_out_shapes(b, l, d):
    return [jax.ShapeDtypeStruct((b, l, d), F32), jax.ShapeDtypeStruct((b, l * (d // LANES), LANES), F32),
            jax.ShapeDtypeStruct((b, N_EXPERTS, l), F32)]


def _even_out(conv, attn, wo_bf, x, g1, nf_g, sh2, sc2, wr_hi, wr_hilo):
    b, l, d = x.shape
    tm = min(512, l)
    tok = lambda bi, t: (bi, t, 0)
    post_in, post_out = _post_specs(tm, d)
    return pl.pallas_call(
        _even_out_kernel,
        grid=(b, l // tm),
        in_specs=[pl.BlockSpec((1, tm, D_CONF), tok), pl.BlockSpec((1, tm, D_ATTN), tok),
                  pl.BlockSpec(wo_bf.shape, lambda bi, t: (0, 0))] + post_in,
        out_specs=post_out,
        out_shape=_post_out_shapes(b, l, d),
        compiler_params=_cparams("arbitrary", "arbitrary"),
        name="even_out_proj",
    )(conv, attn, wo_bf, x, g1, nf_g, sh2, sc2, wr_hi, wr_hilo)


def _odd_in_kernel(x_ref, sh_ref, sc_ref, g_ref, w_ref, z_ref):
    h = _modulated_norm(x_ref[0], g_ref[...], sh_ref[0], sc_ref[0])
    p = _dot(h.astype(BF16), w_ref[...])
    ds_ = D_SHORT
    z_ref[0, :, :ds_] = p[:, 2 * ds_:3 * ds_] * p[:, :ds_]
    z_ref[0, :, ds_:2 * ds_] = p[:, 3 * ds_:]
    z_ref[0, :, 2 * ds_:] = p[:, ds_:2 * ds_]


def _odd_in(x, sh, sc, g, w_bf):
    b, l, d = x.shape
    tm = min(512, l)
    row = lambda bi, t: (bi, 0, 0)
    tok = lambda bi, t: (bi, t, 0)
    const = lambda bi, t: (0, 0)
    nz = 2 * D_SHORT + D_POOL
    return pl.pallas_call(
        _odd_in_kernel,
        grid=(b, l // tm),
        in_specs=[pl.BlockSpec((1, tm, d), tok), pl.BlockSpec((1, 1, d), row), pl.BlockSpec((1, 1, d), row),
                  pl.BlockSpec((1, d), const), pl.BlockSpec(w_bf.shape, const)],
        out_specs=pl.BlockSpec((1, tm, nz), tok),
        out_shape=jax.ShapeDtypeStruct((b, l, nz), F32),
        compiler_params=_cparams("arbitrary", "arbitrary"),
        name="odd_in_proj",
    )(x, sh, sc, g, w_bf)


def _odd_mix_kernel(zp_ref, zc_ref, zn_ref, cw_ref, pw_ref, ps_ref, wo_ref,
                    x_ref, g1_ref, nfg_ref, sh2_ref, sc2_ref, wrh_ref, wrl_ref,
                    xo_ref, h2_ref, aff_ref, xs_ref, *, tm, nt, n_tok):
    _fill_halo(xs_ref, zp_ref, zc_ref, zn_ref, tm, nt)
    ds_ = D_SHORT
    width = cw_ref.shape[0]
    pad = width // 2
    conv = jnp.zeros((tm, ds_), F32)
    for k in range(width):
        conv = conv + xs_ref[pl.ds(HALO - pad + k, tm), 0:ds_] * cw_ref[k:k + 1, :]
    short = xs_ref[pl.ds(HALO, tm), 2 * ds_:3 * ds_] * conv
    pos = pl.program_id(1) * tm + lax.broadcasted_iota(jnp.int32, (tm, POOL_GROUP), 0)
    pooled = []
    for gi, w in enumerate(POOL_WINDOWS):
        c0 = ds_ + gi * POOL_GROUP
        tot = jnp.zeros((tm, POOL_GROUP), F32)
        for dlt in range(-(w // 2), w // 2):
            tot = tot + xs_ref[pl.ds(HALO + dlt, tm), c0:c0 + POOL_GROUP]
        cnt = (jnp.minimum(pos + w // 2, n_tok) - jnp.maximum(pos - w // 2, 0)).astype(F32)
        diff = tot / cnt - xs_ref[pl.ds(HALO, tm), c0:c0 + POOL_GROUP]
        pooled.append(_dot(diff.astype(BF16), pw_ref[gi]))
    pool = jnp.concatenate(pooled, axis=-1) * ps_ref[...]
    y = _dot(short.astype(BF16), wo_ref[:ds_, :]) + _dot(pool.astype(BF16), wo_ref[ds_:, :])
    _post(x_ref[0], y, g1_ref[0], nfg_ref[...], sh2_ref[0], sc2_ref[0], wrh_ref[...], wrl_ref[...],
          xo_ref, h2_ref, aff_ref)


def _odd_mix(z, conv_w, pool_w_bf, pool_scale, wo_bf, x, g1, nf_g, sh2, sc2, wr_hi, wr_hilo):
    b, l, d = x.shape
    tm = min(256, l)
    nt = l // tm
    nz = z.shape[-1]
    const = lambda bi, t: (0, 0)
    post_in, post_out = _post_specs(tm, d)
    return pl.pallas_call(
        functools.partial(_odd_mix_kernel, tm=tm, nt=nt, n_tok=l),
        grid=(b, nt),
        in_specs=_halo_specs(tm, l, nz) + [pl.BlockSpec(conv_w.shape, const),
                                            pl.BlockSpec(pool_w_bf.shape, lambda bi, t: (0, 0, 0)),
                                            pl.BlockSpec((1, D_POOL), const),
                                            pl.BlockSpec(wo_bf.shape, const)] + post_in,
        out_specs=post_out,
        out_shape=_post_out_shapes(b, l, d),
        scratch_shapes=[pltpu.VMEM((tm + 2 * HALO, nz), F32)],
        compiler_params=_cparams("arbitrary", "arbitrary"),
        name="odd_mixer",
    )(z, z, z, conv_w, pool_w_bf, pool_scale, wo_bf, x, g1, nf_g, sh2, sc2, wr_hi, wr_hilo)


def _route_kernel(aff_ref, tri_ref, pos_ref, *, cap):
    key = lax.bitcast_convert_type(aff_ref[0], jnp.int32)
    n = key.shape[1]
    capf = float(cap)
    thr = jnp.zeros((N_EXPERTS, 1), jnp.int32)
    for bit in range(30, -1, -1):
        cand = thr | (1 << bit)
        cnt = jnp.sum(jnp.where(key >= cand, 1.0, 0.0), axis=1, keepdims=True)
        thr = jnp.where(cnt >= capf, cand, thr)
    n_gt = jnp.sum(jnp.where(key > thr, 1.0, 0.0), axis=1, keepdims=True)
    need = capf - n_gt
    tri = tri_ref[...]
    off_eq = jnp.zeros((N_EXPERTS, 1), F32)
    off_sel = jnp.zeros((N_EXPERTS, 1), F32)
    for j in range(n // LANES):
        kj = key[:, j * LANES:(j + 1) * LANES]
        gt = kj > thr
        eqf = jnp.where(kj == thr, 1.0, 0.0)
        rank = _dot(eqf.astype(BF16), tri) + off_eq - eqf
        self_ = jnp.where(gt, 1.0, jnp.where(rank < need, eqf, 0.0))
        slot = _dot(self_.astype(BF16), tri) + off_sel - 1.0
        pos_ref[0, :, j * LANES:(j + 1) * LANES] = jnp.where(self_ > 0.0, slot, -1.0)
        off_eq = off_eq + jnp.sum(eqf, axis=1, keepdims=True)
        off_sel = off_sel + jnp.sum(self_, axis=1, keepdims=True)


def _route(aff_t, cap):
    b, e, n = aff_t.shape
    tri = jnp.asarray(np.triu(np.ones((LANES, LANES), np.float32)), BF16)
    return pl.pallas_call(
        functools.partial(_route_kernel, cap=cap),
        grid=(b,),
        in_specs=[pl.BlockSpec((1, e, n), lambda bi: (bi, 0, 0)), pl.BlockSpec((LANES, LANES), lambda bi: (0, 0))],
        out_specs=pl.BlockSpec((1, e, n), lambda bi: (bi, 0, 0)),
        out_shape=jax.ShapeDtypeStruct((b, e, n), F32),
        compiler_params=_cparams("arbitrary"),
        name="route",
    )(aff_t, tri)


GATHER_ROWS_PER_STEP = 2048
COMBINE_ROWS_PER_STEP = 1024


def _experts_per_step(cap, rows_per_step=GATHER_ROWS_PER_STEP):
    return max(1, min(N_EXPERTS, rows_per_step // cap))


def _slot_index_kernel(pos_ref, idx_ref, *, cap, sc, eps):
    n = pos_ref.shape[3]
    for ee in range(eps):
        pos = pos_ref[0, ee]
        for c in range(cap // sc):
            slot = (lax.broadcasted_iota(jnp.int32, (sc, LANES), 0) + c * sc).astype(F32)
            tok_acc = jnp.zeros((sc, LANES), F32)
            for j in range(n // LANES):
                tok = (lax.broadcasted_iota(jnp.int32, (1, LANES), 1) + j * LANES).astype(F32)
                tok_acc = tok_acc + jnp.where(slot == pos[:, j * LANES:(j + 1) * LANES], tok, 0.0)
            idx_ref[0, ee, :, c * sc:(c + 1) * sc] = jnp.sum(tok_acc.T, axis=0, keepdims=True).astype(jnp.int32)


def _slot_index(pos, cap):
    b, e, n = pos.shape
    sc = min(LANES, cap)
    eps = _experts_per_step(cap)
    sel = lambda bi, ei: (bi, ei, 0, 0)
    return pl.pallas_call(
        functools.partial(_slot_index_kernel, cap=cap, sc=sc, eps=eps),
        grid=(b, e // eps),
        in_specs=[pl.BlockSpec((1, eps, 1, n), sel)],
        out_specs=pl.BlockSpec((1, eps, 1, cap), sel),
        out_shape=jax.ShapeDtypeStruct((b, e, 1, cap), jnp.int32),
        compiler_params=_cparams("arbitrary", "arbitrary"),
        name="slot_index",
    )(pos.reshape(b, e, 1, n))


def _gather_kernel(idx_ref, h3_ref, xe_ref, tile_ref, *, cap, nk, stride, eps):
    for ee in range(eps):
        for s in range(cap):
            t = idx_ref[0, ee, 0, s]
            tile_ref[ee, pl.ds(s, nk, stride=stride), :] = h3_ref[0, pl.ds(pl.multiple_of(t * nk, nk), nk), :]
        for k in range(nk):
            xe_ref[ee, :, k * LANES:(k + 1) * LANES] = tile_ref[ee, k * stride:k * stride + cap, :].astype(BF16)


def _gather(idx, h3, cap):
    b, e = idx.shape[:2]
    rows = h3.shape[1]
    nk = 8
    d = nk * LANES
    stride = cap + 8
    eps = _experts_per_step(cap)
    return pl.pallas_call(
        functools.partial(_gather_kernel, cap=cap, nk=nk, stride=stride, eps=eps),
        grid=(b, e // eps),
        in_specs=[pl.BlockSpec((1, eps, 1, cap), lambda bi, ei: (bi, ei, 0, 0), memory_space=pltpu.SMEM),
                  pl.BlockSpec((1, rows, LANES), lambda bi, ei: (bi, 0, 0))],
        out_specs=pl.BlockSpec((eps, cap, d), lambda bi, ei: (ei, bi, 0)),
        out_shape=jax.ShapeDtypeStruct((e, b * cap, d), BF16),
        scratch_shapes=[pltpu.VMEM((eps, nk * stride, LANES), F32)],
        compiler_params=_cparams("arbitrary", "arbitrary"),
        name="moe_gather",
    )(idx, h3)


def _ffn_kernel(xe_ref, wg_ref, wu_ref, wd_ref, ye_ref, acc_ref, wg_bf, wu_bf, wd_bf, *, nf, tm):
    f = pl.program_id(1)
    m = pl.program_id(2)

    @pl.when(m == 0)
    def _():
        wg_bf[...] = wg_ref[0, 0].astype(BF16)
        wu_bf[...] = wu_ref[0, 0].astype(BF16)
        wd_bf[...] = wd_ref[0, 0].astype(BF16)

    rows = pl.ds(pl.multiple_of(m * tm, tm), tm)

    def step(first, last):
        x = xe_ref[0]
        g = _dot(x, wg_bf[...])
        u = _dot(x, wu_bf[...])
        tot = _dot((g * _sigmoid(g) * u).astype(BF16), wd_bf[...])
        if not first:
            tot = acc_ref[rows, :] + tot
        if last:
            _store_token_major(ye_ref, tot)
        else:
            acc_ref[rows, :] = tot

    pl.when(f == 0)(lambda: step(True, False))
    pl.when(jnp.logical_and(f > 0, f < nf - 1))(lambda: step(False, False))
    pl.when(f == nf - 1)(lambda: step(False, True))


def _expert_ffn(xe, w_gate, w_up, w_down, layer):
    e, m, d = xe.shape
    dff = w_gate.shape[-1]
    nk = d // LANES
    tm = min(1024, m)
    tf = 512
    nf = dff // tf
    out_tile = lambda ei, f, mi: (ei, jnp.where(f == nf - 1, mi, 0), 0)
    return pl.pallas_call(
        functools.partial(_ffn_kernel, nf=nf, tm=tm),
        grid=(e, nf, m // tm),
        in_specs=[pl.BlockSpec((1, tm, d), lambda ei, f, mi: (ei, mi, 0)),
                  pl.BlockSpec((1, 1, d, tf), lambda ei, f, mi: (layer, ei, 0, f)),
                  pl.BlockSpec((1, 1, d, tf), lambda ei, f, mi: (layer, ei, 0, f)),
                  pl.BlockSpec((1, 1, tf, d), lambda ei, f, mi: (layer, ei, f, 0))],
        out_specs=pl.BlockSpec((1, tm * nk, LANES), out_tile),
        out_shape=jax.ShapeDtypeStruct((e, m * nk, LANES), F32),
        scratch_shapes=[pltpu.VMEM((m, d), F32), pltpu.VMEM((d, tf), BF16), pltpu.VMEM((d, tf), BF16),
                        pltpu.VMEM((tf, d), BF16)],
        compiler_params=_cparams("arbitrary", "arbitrary", "arbitrary"),
        name="expert_ffn",
    )(xe, w_gate, w_up, w_down)


def _combine_kernel(idx_ref, aff_ref, ye_ref, x_ref, g2_ref, o_ref, acc_ref, *, cap, nk, tm, batch, eps, n_scatter):
    k = pl.program_id(1)

    @pl.when(k == 0)
    def _():
        acc_ref[...] = jnp.zeros_like(acc_ref)

    @pl.when(k < n_scatter)
    def _():
        for ee in range(eps):
            for s0 in range(0, cap, batch):
                rows = []
                for s in range(s0, s0 + batch):
                    t = idx_ref[0, ee, 0, s]
                    start = pl.multiple_of(t * nk, nk)
                    rows.append((start, acc_ref[pl.ds(start, nk), :]
                                 + aff_ref[0, ee, 0, t] * ye_ref[ee, s * nk:(s + 1) * nk, :]))
                for start, val in rows:
                    acc_ref[pl.ds(start, nk), :] = val

    @pl.when(k >= n_scatter)
    def _():
        view = acc_ref.at[pl.ds(pl.multiple_of((k - n_scatter) * (tm * nk), tm * nk), tm * nk), :]
        for c in range(nk):
            lanes = slice(c * LANES, (c + 1) * LANES)
            o_ref[0, :, lanes] = x_ref[0, :, lanes] + g2_ref[0, :, lanes] * view[pl.ds(c, tm, stride=nk), :]


def _combine(idx, aff_t, ye3, x, g2, cap):
    b, n, d = x.shape
    e = idx.shape[1]
    nk = d // LANES
    tm = min(1024, n)
    nt = n // tm
    eps = _experts_per_step(cap, COMBINE_ROWS_PER_STEP)
    ns = e // eps
    group = lambda bi, k: (bi, jnp.minimum(k, ns - 1), 0, 0)
    smem = lambda width: pl.BlockSpec((1, eps, 1, width), group, memory_space=pltpu.SMEM)
    tile = lambda bi, k: (bi, jnp.maximum(k - ns, 0), 0)
    return pl.pallas_call(
        functools.partial(_combine_kernel, cap=cap, nk=nk, tm=tm, batch=8, eps=eps, n_scatter=ns),
        grid=(b, ns + nt),
        in_specs=[smem(cap), smem(n),
                  pl.BlockSpec((eps, cap * nk, LANES), lambda bi, k: (jnp.minimum(k, ns - 1), bi, 0)),
                  pl.BlockSpec((1, tm, d), tile),
                  pl.BlockSpec((1, 1, d), lambda bi, k: (bi, 0, 0))],
        out_specs=pl.BlockSpec((1, tm, d), tile),
        out_shape=jax.ShapeDtypeStruct((b, n, d), F32),
        scratch_shapes=[pltpu.VMEM((n * nk, LANES), F32)],
        compiler_params=_cparams("arbitrary", "arbitrary"),
        name="moe_combine",
    )(idx, aff_t.reshape(b, e, 1, n), ye3, x, g2)


def _moe(x, h3, aff_t, g2, w_gate, w_up, w_down, layer):
    cap = CAPACITY_FACTOR * x.shape[1] // N_EXPERTS
    idx = _slot_index(_route(aff_t, cap), cap)
    ye3 = _expert_ffn(_gather(idx, h3, cap), w_gate, w_up, w_down, layer)
    return _combine(idx, aff_t, ye3, x, g2, cap)


def _rope_tables(n_tok):
    rows = n_tok // GRID_W
    row = np.repeat(np.arange(rows), GRID_W).astype(np.float32)
    col = np.tile(np.arange(GRID_W), rows).astype(np.float32)
    n_freq = HEAD_DIM // 4
    inv = jnp.asarray(ROPE_THETA, F32) ** (-jnp.arange(n_freq, dtype=F32) / n_freq)
    ang_r = jnp.asarray(row)[:, None] * inv
    ang_c = jnp.asarray(col)[:, None] * inv
    cr, sr, cc, sc = jnp.cos(ang_r), jnp.sin(ang_r), jnp.cos(ang_c), jnp.sin(ang_c)
    zero = jnp.zeros_like(sr)
    cos = jnp.concatenate([cr, cr, cc, cc], axis=-1)
    sin_a = jnp.concatenate([-sr, zero, -sc, zero], axis=-1)
    sin_b = jnp.concatenate([zero, sr, zero, sc], axis=-1)
    return tuple(jnp.tile(t, (1, N_Q_HEADS)) for t in (cos, sin_a, sin_b))


def _head_mean_matrix():
    blk = np.kron(np.eye(N_Q_HEADS, dtype=np.float32), np.full((HEAD_DIM, HEAD_DIM), 1.0 / HEAD_DIM, np.float32))
    return jnp.asarray(blk, BF16)


def _rows(m, b):
    return m[:b, None, :]


def kernel(x, c, ctx, c_ctx, norm_mix_g, norm_ffn_g, w_mod, b_mod, ev_w_in, ev_conv_w, ev_conv_b, ev_ln_g, ev_ln_b, ev_q_norm_g, ev_k_norm_g, ev_w_out, od_w_in, od_conv_w, od_pool_w, od_pool_scale, od_w_out, w_router, w_gate, w_up, w_down):
    b, l, d = x.shape
    depth = w_mod.shape[0]
    last_even = ((depth - 1) // 2) * 2

    n_rows = -(-(b + 1) // 8) * 8
    c_rows = jnp.zeros((n_rows, d), F32).at[:b].set(c).at[b].set(c_ctx)
    mods = _modulations(c_rows, w_mod, b_mod)
    rope = _rope_tables(l)
    bd = _head_mean_matrix()

    for i in range(depth):
        j = i // 2
        is_even = i % 2 == 0
        ctx_live = i < last_even
        m6 = mods[i].reshape(n_rows, 6, d)
        sh1, sc1, g1, sh2, sc2, g2 = [_rows(m6[:, t], b) for t in range(6)]
        mc = [jnp.broadcast_to(m6[b, t][None, None, :], (b, 1, d)) for t in range(6)]
        sh1c, sc1c, g1c, sh2c, sc2c, g2c = mc
        nm_g = norm_mix_g[i][None, :]
        nf_g = norm_ffn_g[i][None, :]
        wr_hi, wr_low = _split_bf16(jnp.pad(w_router[i], ((0, 0), (0, LANES - N_EXPERTS))))
        wr_hilo = jnp.concatenate([wr_hi, wr_low], axis=1)

        if is_even:
            w_in = ev_w_in[j].astype(BF16)
            wo = ev_w_out[j].astype(BF16)
            qg = jnp.tile(ev_q_norm_g[j], N_Q_HEADS)[None, :]
            kg = jnp.tile(ev_k_norm_g[j], N_KV_HEADS)[None, :]
            conv_args = (ev_conv_w[j], ev_conv_b[j][None, :], ev_ln_g[j][None, :], ev_ln_b[j][None, :])
            u, q, k, v = _even_in(x, sh1, sc1, nm_g, w_in, qg, kg, bd, rope)
            uc, qc, kc, vc = _even_in(ctx, sh1c, sc1c, nm_g, w_in, qg, kg, bd, None)
            attn = _attention(q, jnp.concatenate([kc, k], axis=1), jnp.concatenate([vc, v], axis=1))
            conv = _conf_conv(u, *conv_args)
            x, h3, aff = _even_out(conv, attn, wo, x, g1, nf_g, sh2, sc2, wr_hi, wr_hilo)
            if ctx_live:
                attn_c = _attention(qc, kc, vc)
                conv_c = _conf_conv(uc, *conv_args)
                ctx, h3c, affc = _even_out(conv_c, attn_c, wo, ctx, g1c, nf_g, sh2c, sc2c, wr_hi, wr_hilo)
        else:
            w_in = od_w_in[j].astype(BF16)
            wo = od_w_out[j].astype(BF16)
            mix_args = (od_conv_w[j], od_pool_w[j].astype(BF16), od_pool_scale[j][None, :], wo)
            z = _odd_in(x, sh1, sc1, nm_g, w_in)
            x, h3, aff = _odd_mix(z, *mix_args, x, g1, nf_g, sh2, sc2, wr_hi, wr_hilo)
            if ctx_live:
                zc = _odd_in(ctx, sh1c, sc1c, nm_g, w_in)
                ctx, h3c, affc = _odd_mix(zc, *mix_args, ctx, g1c, nf_g, sh2c, sc2c, wr_hi, wr_hilo)
        x = _moe(x, h3, aff, g2, w_gate, w_up, w_down, i)
        if ctx_live:
            ctx = _moe(ctx, h3c, affc, g2c, w_gate, w_up, w_down, i)
    return x
```
